```python
import math
import jax, jax.numpy as jnp
from jax import lax
import numpy as np

D_MODEL = 1024
BATCH = 4
SEQ = 4096
DEPTH = 4

PLE_DIM = 256
D_FF = 2816
EPS = 1e-6

GLA_HEADS = 4
GLA_DK = 64
GLA_DV = 128
GLA_KW = GLA_HEADS * GLA_DK
GLA_VW = GLA_HEADS * GLA_DV
GLA_GATE_RANK = 16
GLA_GATE_TAU = 16.0
GLA_CHUNK = 64
LRU_WIDTH = 512
LRU_BLOCKS = 8
LRU_BLOCK_W = LRU_WIDTH // LRU_BLOCKS
LRU_CONV_W = 4
LRU_C = 8.0
HYB_IN = 2 * GLA_KW + 2 * GLA_VW + GLA_GATE_RANK + 2 * LRU_WIDTH
HYB_MIX = GLA_VW + LRU_WIDTH

SWA_HEADS = 16
SWA_KV_HEADS = 4
SWA_HEAD_DIM = 64
SWA_GROUP = SWA_HEADS // SWA_KV_HEADS
SWA_WINDOW = 128
SWA_BLOCK = 128
SWA_QKV = (SWA_HEADS + 2 * SWA_KV_HEADS) * SWA_HEAD_DIM

REL_BUCKETS = 32
REL_MAX_DIST = 128

N_EVEN = (DEPTH + 1) // 2
N_ODD = DEPTH // 2

kernel_name = "hybrid_gla_rglru_swa_macaron"


def rms_norm(x, g):
    xf = x.astype(jnp.float32)
    y = xf * lax.rsqrt(jnp.mean(xf * xf, axis=-1, keepdims=True) + EPS)
    return (y * g.astype(jnp.float32)).astype(x.dtype)


def swiglu(x, w_gate, w_up, w_down):
    return (jax.nn.silu(x @ w_gate) * (x @ w_up)) @ w_down


def t5_bucket(dist):
    max_exact = REL_BUCKETS // 2
    d = jnp.maximum(dist, 1).astype(jnp.float32)
    large = max_exact + (jnp.log(d / max_exact) / math.log(REL_MAX_DIST / max_exact)
                         * (REL_BUCKETS - max_exact)).astype(jnp.int32)
    large = jnp.minimum(large, REL_BUCKETS - 1)
    return jnp.where(dist < max_exact, dist, large)


def gla(q, k, v, log_f, r, norm_g):
    B, S, _ = q.shape
    N = S // GLA_CHUNK
    C = GLA_CHUNK

    def split(t, d):
        return t.reshape(B, N, C, GLA_HEADS, d).transpose(0, 3, 1, 2, 4).astype(jnp.float32)

    qc = split(q, GLA_DK) * (GLA_DK ** -0.5)
    kc = split(k, GLA_DK)
    vc = split(v, GLA_DV)
    b = jnp.cumsum(split(log_f, GLA_DK), axis=3)
    b_last = b[..., -1:, :]
    q_dec = qc * jnp.exp(b)
    k_dec = kc * jnp.exp(-b)
    causal = jnp.tril(jnp.ones((C, C), dtype=bool))
    att = jnp.where(causal, jnp.einsum('bhnik,bhnjk->bhnij', q_dec, k_dec), 0.0)
    o_intra = jnp.einsum('bhnij,bhnjv->bhniv', att, vc)
    kv = jnp.einsum('bhnjk,bhnjv->bhnkv', kc * jnp.exp(b_last - b), vc)
    decay = jnp.exp(b_last[..., 0, :])

    def step(state, inp):
        d, u = inp
        return d[..., None] * state + u, state

    s0 = jnp.zeros((B, GLA_HEADS, GLA_DK, GLA_DV), jnp.float32)
    _, s_prev = lax.scan(step, s0, (jnp.moveaxis(decay, 2, 0), jnp.moveaxis(kv, 2, 0)))
    s_prev = jnp.moveaxis(s_prev, 0, 2)
    o = o_intra + jnp.einsum('bhnik,bhnkv->bhniv', q_dec, s_prev)
    o = o * lax.rsqrt(jnp.mean(o * o, axis=-1, keepdims=True) + EPS)
    o = o.transpose(0, 2, 3, 1, 4).reshape(B, S, GLA_VW)
    o = o * norm_g.astype(jnp.float32) * jax.nn.silu(r.astype(jnp.float32))
    return o.astype(v.dtype)


def causal_dwconv(x, w, b):
    S = x.shape[1]
    K = w.shape[0]
    xp = jnp.pad(x, ((0, 0), (K - 1, 0), (0, 0)))
    out = xp[:, 0:S] * w[0]
    for tap in range(1, K):
        out = out + xp[:, tap:tap + S] * w[tap]
    return out + b


def rg_lru(x, w_a, b_a, w_x, b_x, lam):
    B, S, _ = x.shape
    xf = x.astype(jnp.float32)
    xb = xf.reshape(B, S, LRU_BLOCKS, LRU_BLOCK_W)
    r = jax.nn.sigmoid(jnp.einsum('bsgi,gij->bsgj', xb, w_a.astype(jnp.float32)).reshape(B, S, LRU_WIDTH) + b_a)
    i = jax.nn.sigmoid(jnp.einsum('bsgi,gij->bsgj', xb, w_x.astype(jnp.float32)).reshape(B, S, LRU_WIDTH) + b_x)
    log_a = -LRU_C * r * jax.nn.softplus(-lam.astype(jnp.float32))
    a = jnp.exp(log_a)
    u = jnp.sqrt(-jnp.expm1(2.0 * log_a)) * (i * xf)

    def combine(left, right):
        a1, b1 = left
        a2, b2 = right
        return a1 * a2, a2 * b1 + b2

    _, h = lax.associative_scan(combine, (a, u), axis=1)
    return h.astype(x.dtype)


def hybrid_mixer(h, w_in, w_out, gla_w_fup, gla_b_f, gla_norm, conv_w, conv_b,
                 lru_w_a, lru_b_a, lru_w_x, lru_b_x, lru_lambda):
    z = h @ w_in
    sizes = (GLA_KW, GLA_KW, GLA_VW, GLA_VW, GLA_GATE_RANK, LRU_WIDTH, LRU_WIDTH)
    idx = [int(c) for c in np.cumsum(sizes)[:-1]]
    q, k, v, r, f_low, xr, gr = jnp.split(z, idx, axis=-1)
    log_f = jax.nn.log_sigmoid((f_low @ gla_w_fup + gla_b_f).astype(jnp.float32)) / GLA_GATE_TAU
    o_a = gla(q, k, v, log_f, r, gla_norm)
    xr = causal_dwconv(xr, conv_w, conv_b)
    o_b = rg_lru(xr, lru_w_a, lru_b_a, lru_w_x, lru_b_x, lru_lambda) * jax.nn.gelu(gr)
    return jnp.concatenate([o_a, o_b], axis=-1) @ w_out


def swa(h, w_qkv, b_qkv, w_o, b_o, sinks, rel_bias):
    B, S, _ = h.shape
    NB = S // SWA_BLOCK
    BLK = SWA_BLOCK
    z = h @ w_qkv + b_qkv
    q, k, v = jnp.split(z, [SWA_HEADS * SWA_HEAD_DIM, (SWA_HEADS + SWA_KV_HEADS) * SWA_HEAD_DIM], axis=-1)
    q = q.reshape(B, NB, BLK, SWA_KV_HEADS, SWA_GROUP, SWA_HEAD_DIM)
    k = k.reshape(B, S, SWA_KV_HEADS, SWA_HEAD_DIM)
    v = v.reshape(B, S, SWA_KV_HEADS, SWA_HEAD_DIM)

    def band(t):
        prev = jnp.pad(t, ((0, 0), (BLK, 0), (0, 0), (0, 0)))[:, :S]
        return jnp.concatenate([prev.reshape(B, NB, BLK, SWA_KV_HEADS, SWA_HEAD_DIM),
                                t.reshape(B, NB, BLK, SWA_KV_HEADS, SWA_HEAD_DIM)], axis=2)

    kb, vb = band(k), band(v)
    scores = jnp.einsum('bnqhgd,bnkhd->bnhgqk', q, kb).astype(jnp.float32) * (SWA_HEAD_DIM ** -0.5)
    qi = jnp.arange(BLK)[:, None] + BLK
    kj = jnp.arange(2 * BLK)[None, :]
    dist = qi - kj
    bias = rel_bias.astype(jnp.float32)[t5_bucket(jnp.maximum(dist, 0))]
    bias = bias.transpose(2, 0, 1).reshape(SWA_KV_HEADS, SWA_GROUP, BLK, 2 * BLK)
    key_pos = jnp.arange(NB)[:, None] * BLK - BLK + kj
    valid = (dist >= 0) & (dist < SWA_WINDOW) & (key_pos[:, None, :] >= 0)
    scores = jnp.where(valid[None, :, None, None], scores + bias, -1e30)
    sink = jnp.broadcast_to(sinks.astype(jnp.float32).reshape(1, 1, SWA_KV_HEADS, SWA_GROUP, 1, 1),
                            scores.shape[:-1] + (1,))
    probs = jax.nn.softmax(jnp.concatenate([scores, sink], axis=-1), axis=-1)[..., :-1]
    o = jnp.einsum('bnhgqk,bnkhd->bnqhgd', probs.astype(v.dtype), vb)
    o = o.reshape(B, S, SWA_HEADS * SWA_HEAD_DIM)
    return o @ w_o + b_o


def setup_inputs(seed: int = 0) -> dict:
    key = jax.random.key(seed)
    ks = iter(jax.random.split(key, 48))

    def nrm(shape, scale):
        return jax.random.normal(next(ks), shape, jnp.float32) * scale

    def gain(shape):
        return 1.0 + nrm(shape, 0.02)

    a_init = jax.random.uniform(next(ks), (N_EVEN, LRU_WIDTH), jnp.float32, 0.9, 0.999)
    return {
        "x": nrm((BATCH, SEQ, D_MODEL), 1.0),
        "p": nrm((DEPTH, BATCH, SEQ, PLE_DIM), 1.0),
        "rel_bias": nrm((REL_BUCKETS, SWA_HEADS), 0.5),
        "final_norm": gain((D_MODEL,)),
        "ffn1_norm": gain((DEPTH, D_MODEL)),
        "ffn1_w_gate": nrm((DEPTH, D_MODEL, D_FF), D_MODEL ** -0.5),
        "ffn1_w_up": nrm((DEPTH, D_MODEL, D_FF), D_MODEL ** -0.5),
        "ffn1_w_down": nrm((DEPTH, D_FF, D_MODEL), D_FF ** -0.5),
        "mix_norm": gain((DEPTH, D_MODEL)),
        "ffn2_norm": gain((DEPTH, D_MODEL)),
        "ffn2_w_gate": nrm((DEPTH, D_MODEL, D_FF), D_MODEL ** -0.5),
        "ffn2_w_up": nrm((DEPTH, D_MODEL, D_FF), D_MODEL ** -0.5),
        "ffn2_w_down": nrm((DEPTH, D_FF, D_MODEL), D_FF ** -0.5),
        "ple_norm": gain((DEPTH, D_MODEL)),
        "ple_w_proj": nrm((DEPTH, PLE_DIM, D_MODEL), PLE_DIM ** -0.5),
        "ple_w_gate": nrm((DEPTH, D_MODEL, D_MODEL), D_MODEL ** -0.5),
        "hyb_w_in": nrm((N_EVEN, D_MODEL, HYB_IN), D_MODEL ** -0.5),
        "hyb_w_out": nrm((N_EVEN, HYB_MIX, D_MODEL), HYB_MIX ** -0.5),
        "gla_w_fup": nrm((N_EVEN, GLA_GATE_RANK, GLA_KW), GLA_GATE_RANK ** -0.5),
        "gla_b_f": nrm((N_EVEN, GLA_KW), 0.1),
        "gla_norm": gain((N_EVEN, GLA_VW)),
        "lru_conv_w": nrm((N_EVEN, LRU_CONV_W, LRU_WIDTH), LRU_CONV_W ** -0.5),
        "lru_conv_b": nrm((N_EVEN, LRU_WIDTH), 0.02),
        "lru_w_a": nrm((N_EVEN, LRU_BLOCKS, LRU_BLOCK_W, LRU_BLOCK_W), LRU_BLOCK_W ** -0.5),
        "lru_b_a": nrm((N_EVEN, LRU_WIDTH), 0.02),
        "lru_w_x": nrm((N_EVEN, LRU_BLOCKS, LRU_BLOCK_W, LRU_BLOCK_W), LRU_BLOCK_W ** -0.5),
        "lru_b_x": nrm((N_EVEN, LRU_WIDTH), 0.02),
        "lru_lambda": jnp.log(a_init) - jnp.log1p(-a_init),
        "swa_w_qkv": nrm((N_ODD, D_MODEL, SWA_QKV), D_MODEL ** -0.5),
        "swa_b_qkv": nrm((N_ODD, SWA_QKV), 0.02),
        "swa_w_o": nrm((N_ODD, SWA_HEADS * SWA_HEAD_DIM, D_MODEL), (SWA_HEADS * SWA_HEAD_DIM) ** -0.5),
        "swa_b_o": nrm((N_ODD, D_MODEL), 0.02),
        "swa_sinks": nrm((N_ODD, SWA_HEADS), 0.5),
    }


def reference(x, p, rel_bias, final_norm,
              ffn1_norm, ffn1_w_gate, ffn1_w_up, ffn1_w_down,
              mix_norm,
              ffn2_norm, ffn2_w_gate, ffn2_w_up, ffn2_w_down,
              ple_norm, ple_w_proj, ple_w_gate,
              hyb_w_in, hyb_w_out, gla_w_fup, gla_b_f, gla_norm,
              lru_conv_w, lru_conv_b, lru_w_a, lru_b_a, lru_w_x, lru_b_x, lru_lambda,
              swa_w_qkv, swa_b_qkv, swa_w_o, swa_b_o, swa_sinks):
    for i in range(DEPTH):
        x = x + 0.5 * swiglu(rms_norm(x, ffn1_norm[i]), ffn1_w_gate[i], ffn1_w_up[i], ffn1_w_down[i])
        h = rms_norm(x, mix_norm[i])
        if i % 2 == 0:
            e = i // 2
            x = x + hybrid_mixer(h, hyb_w_in[e], hyb_w_out[e], gla_w_fup[e], gla_b_f[e], gla_norm[e],
                                 lru_conv_w[e], lru_conv_b[e], lru_w_a[e], lru_b_a[e],
                                 lru_w_x[e], lru_b_x[e], lru_lambda[e])
        else:
            o = i // 2
            x = x + swa(h, swa_w_qkv[o], swa_b_qkv[o], swa_w_o[o], swa_b_o[o], swa_sinks[o], rel_bias)
        x = x + 0.5 * swiglu(rms_norm(x, ffn2_norm[i]), ffn2_w_gate[i], ffn2_w_up[i], ffn2_w_down[i])
        gate = jax.nn.sigmoid(rms_norm(x, ple_norm[i]) @ ple_w_gate[i])
        x = x + gate * (p[i] @ ple_w_proj[i])
    return rms_norm(x, final_norm)
```

```python
import functools
import math

import jax
import jax.numpy as jnp
import numpy as np
from jax import lax
from jax.experimental import pallas as pl
from jax.experimental.pallas import tpu as pltpu

F32 = jnp.float32
BF16 = jnp.bfloat16

D_MODEL = 1024
DEPTH = 4
PLE_DIM = 256
D_FF = 2816
EPS = 1e-6

GLA_HEADS = 4
GLA_DK = 64
GLA_DV = 128
GLA_KW = GLA_HEADS * GLA_DK
GLA_VW = GLA_HEADS * GLA_DV
GLA_GATE_RANK = 16
GLA_GATE_TAU = 16.0
GLA_CHUNK = 64
LRU_WIDTH = 512
LRU_BLOCKS = 8
LRU_CONV_W = 4
LRU_C = 8.0

SWA_HEADS = 16
SWA_KV_HEADS = 4
SWA_HEAD_DIM = 64
SWA_GROUP = SWA_HEADS // SWA_KV_HEADS
SWA_BLOCK = 128
SWA_Q_W = SWA_HEADS * SWA_HEAD_DIM
SWA_KV_W = SWA_KV_HEADS * SWA_HEAD_DIM
REL_BUCKETS = 32
REL_MAX_DIST = 128
MASK_VALUE = -1e30

LANES = 128
SUBLANES = 8
VMEM_LIMIT_BYTES = 56 * 1024 * 1024

FFN_TM = 1024
FFN_TF = 256
PROJ_TM = 512
HYB_TS = 256

HYB_QP = GLA_HEADS * LANES
Z_Q = 0
Z_K = Z_Q + HYB_QP
Z_V = Z_K + HYB_QP
Z_R = Z_V + GLA_VW
Z_X = Z_R + GLA_VW
Z_G = Z_X + LRU_WIDTH
Z_F = Z_G + LRU_WIDTH
HYB_ZW = Z_F + LANES


def _params(*semantics):
    return pltpu.CompilerParams(dimension_semantics=semantics, vmem_limit_bytes=VMEM_LIMIT_BYTES)


def _rms(x, g):
    return x * lax.rsqrt(jnp.mean(x * x, axis=-1, keepdims=True) + EPS) * g


def _sigmoid(x):
    return 1.0 / (1.0 + jnp.exp(-x))


def _dot(a, b):
    return jnp.dot(a, b, preferred_element_type=F32)


def _dot_nt(a, b):
    return lax.dot_general(a, b, (((1,), (1,)), ((), ())), preferred_element_type=F32)


def _dot_tn(a, b):
    return lax.dot_general(a, b, (((0,), (0,)), ((), ())), preferred_element_type=F32)


def _ffn_body(x_ref, g_ref, wg_ref, wu_ref, wd_ref, o_ref, xn_ref):
    @pl.when(pl.program_id(1) == 0)
    def _():
        x = x_ref[...]
        xn_ref[...] = _rms(x, g_ref[...]).astype(BF16)
        o_ref[...] = x

    xn = xn_ref[...]
    a = _dot(xn, wg_ref[...].astype(BF16))
    b = _dot(xn, wu_ref[...].astype(BF16))
    h = (0.5 * a) * _sigmoid(a) * b
    o_ref[...] += _dot(h.astype(BF16), wd_ref[...].astype(BF16))


def _ffn(x, g, w_gate, w_up, w_down, layer):
    t = x.shape[0]
    return pl.pallas_call(
        _ffn_body,
        grid=(t // FFN_TM, D_FF // FFN_TF),
        in_specs=[
            pl.BlockSpec((FFN_TM, D_MODEL), lambda i, j: (i, 0)),
            pl.BlockSpec((None, 1, D_MODEL), lambda i, j: (layer, 0, 0)),
            pl.BlockSpec((None, D_MODEL, FFN_TF), lambda i, j: (layer, 0, j)),
            pl.BlockSpec((None, D_MODEL, FFN_TF), lambda i, j: (layer, 0, j)),
            pl.BlockSpec((None, FFN_TF, D_MODEL), lambda i, j: (layer, j, 0)),
        ],
        out_specs=pl.BlockSpec((FFN_TM, D_MODEL), lambda i, j: (i, 0)),
        out_shape=jax.ShapeDtypeStruct((t, D_MODEL), F32),
        scratch_shapes=[pltpu.VMEM((FFN_TM, D_MODEL), BF16)],
        compiler_params=_params("parallel", "arbitrary"),
        name="ffn",
    )(x, g.reshape(DEPTH, 1, D_MODEL), w_gate, w_up, w_down)


def _norm_proj_body(x_ref, g_ref, w_ref, b_ref, o_ref):
    xn = _rms(x_ref[...], g_ref[...]).astype(BF16)
    o_ref[...] = (_dot(xn, w_ref[...]) + b_ref[...]).astype(o_ref.dtype)


def _norm_proj(x, g, w, b, out_dtype, name):
    t = x.shape[0]
    n = w.shape[1]
    return pl.pallas_call(
        _norm_proj_body,
        grid=(t // PROJ_TM,),
        in_specs=[
            pl.BlockSpec((PROJ_TM, D_MODEL), lambda i: (i, 0)),
            pl.BlockSpec((1, D_MODEL), lambda i: (0, 0)),
            pl.BlockSpec((D_MODEL, n), lambda i: (0, 0)),
            pl.BlockSpec((1, n), lambda i: (0, 0)),
        ],
        out_specs=pl.BlockSpec((PROJ_TM, n), lambda i: (i, 0)),
        out_shape=jax.ShapeDtypeStruct((t, n), out_dtype),
        compiler_params=_params("parallel"),
        name=name,
    )(x, g.reshape(1, D_MODEL), w, b.reshape(1, n))


def _proj_res_body(a_ref, w_ref, b_ref, x_ref, o_ref):
    o_ref[...] = x_ref[...] + (_dot(a_ref[...], w_ref[...]) + b_ref[...])


def _proj_res(a, w, b, x, name):
    t, k = a.shape
    return pl.pallas_call(
        _proj_res_body,
        grid=(t // PROJ_TM,),
        in_specs=[
            pl.BlockSpec((PROJ_TM, k), lambda i: (i, 0)),
            pl.BlockSpec((k, D_MODEL), lambda i: (0, 0)),
            pl.BlockSpec((1, D_MODEL), lambda i: (0, 0)),
            pl.BlockSpec((PROJ_TM, D_MODEL), lambda i: (i, 0)),
        ],
        out_specs=pl.BlockSpec((PROJ_TM, D_MODEL), lambda i: (i, 0)),
        out_shape=jax.ShapeDtypeStruct((t, D_MODEL), F32),
        compiler_params=_params("parallel"),
        name=name,
    )(a, w, b.reshape(1, D_MODEL), x)


def _ple_body(x_ref, g_ref, wg_ref, p_ref, wp_ref, fg_ref, o_ref, *, final):
    x = x_ref[...]
    gate = _sigmoid(_dot(_rms(x, g_ref[...]).astype(BF16), wg_ref[...]))
    y = x + gate * _dot(p_ref[...].astype(BF16), wp_ref[...])
    if final:
        y = _rms(y, fg_ref[...])
    o_ref[...] = y


def _ple(x, g, w_gate, p, w_proj, final_g, final):
    t = x.shape[0]
    return pl.pallas_call(
        functools.partial(_ple_body, final=final),
        grid=(t // PROJ_TM,),
        in_specs=[
            pl.BlockSpec((PROJ_TM, D_MODEL), lambda i: (i, 0)),
            pl.BlockSpec((1, D_MODEL), lambda i: (0, 0)),
            pl.BlockSpec((D_MODEL, D_MODEL), lambda i: (0, 0)),
            pl.BlockSpec((PROJ_TM, PLE_DIM), lambda i: (i, 0)),
            pl.BlockSpec((PLE_DIM, D_MODEL), lambda i: (0, 0)),
            pl.BlockSpec((1, D_MODEL), lambda i: (0, 0)),
        ],
        out_specs=pl.BlockSpec((PROJ_TM, D_MODEL), lambda i: (i, 0)),
        out_shape=jax.ShapeDtypeStruct((t, D_MODEL), F32),
        compiler_params=_params("parallel"),
        name="ple",
    )(x, g.reshape(1, D_MODEL), w_gate, p, w_proj, final_g.reshape(1, D_MODEL))


def _split3(x):
    hi = x.astype(BF16)
    r1 = x - hi.astype(F32)
    mid = r1.astype(BF16)
    lo = (r1 - mid.astype(F32)).astype(BF16)
    return hi, mid, lo


def _hybrid_body(z_ref, wf_ref, bf_ref, gn_ref, cw_ref, cb_ref, wa_ref, ba_ref, wx_ref, bx_ref, lam_ref,
                 o_ref, st_ref, xbuf_ref, a_ref, u_ref, h_ref, hc_ref):
    ts = HYB_TS
    c_len = GLA_CHUNK

    @pl.when(pl.program_id(1) == 0)
    def _():
        st_ref[...] = jnp.zeros_like(st_ref)
        xbuf_ref[0:SUBLANES, :] = jnp.zeros((SUBLANES, LRU_WIDTH), F32)
        hc_ref[...] = jnp.zeros_like(hc_ref)

    f_low = z_ref[:, Z_F:Z_F + LANES].astype(BF16)
    gate_in = _dot(f_low, wf_ref[...]) + bf_ref[...]
    log_f = (jnp.minimum(gate_in, 0.0) - jnp.log1p(jnp.exp(-jnp.abs(gate_in)))) * (1.0 / GLA_GATE_TAU)
    row = lax.broadcasted_iota(jnp.int32, (ts, ts), 0)
    col = lax.broadcasted_iota(jnp.int32, (ts, ts), 1)
    tri = jnp.where((row // c_len == col // c_len) & (col <= row), 1.0, 0.0).astype(BF16)
    hi, mid, lo = _split3(log_f)
    b_all = _dot(tri, hi) + _dot(tri, mid) + _dot(tri, lo)

    crow = lax.broadcasted_iota(jnp.int32, (c_len, c_len), 0)
    ccol = lax.broadcasted_iota(jnp.int32, (c_len, c_len), 1)
    causal = ccol <= crow
    for c in range(ts // c_len):
        r0 = c * c_len
        b_c = b_all[r0:r0 + c_len]
        b_last = b_c[c_len - 1:c_len]
        q_c = z_ref[r0:r0 + c_len, Z_Q:Z_Q + HYB_QP]
        k_c = z_ref[r0:r0 + c_len, Z_K:Z_K + HYB_QP]
        q_dec = (q_c * (GLA_DK ** -0.5) * jnp.exp(b_c)).astype(BF16)
        k_dec = (k_c * jnp.exp(-b_c)).astype(BF16)
        k_end = (k_c * jnp.exp(b_last - b_c)).astype(BF16)
        decay = jnp.exp(b_last)
        v_c = z_ref[r0:r0 + c_len, Z_V:Z_V + GLA_VW].astype(BF16)
        r_c = z_ref[r0:r0 + c_len, Z_R:Z_R + GLA_VW]
        for hd in range(GLA_HEADS):
            kl = slice(hd * LANES, (hd + 1) * LANES)
            vl = slice(hd * GLA_DV, (hd + 1) * GLA_DV)
            att = jnp.where(causal, _dot_nt(q_dec[:, kl], k_dec[:, kl]), 0.0)
            s_t = st_ref[hd]
            o = _dot(att.astype(BF16), v_c[:, vl]) + _dot_nt(q_dec[:, kl], s_t.astype(BF16))
            st_ref[hd] = s_t * decay[:, kl] + _dot_tn(v_c[:, vl], k_end[:, kl])
            o = o * lax.rsqrt(jnp.mean(o * o, axis=-1, keepdims=True) + EPS)
            r_h = r_c[:, vl]
            o = o * gn_ref[:, vl] * (r_h * _sigmoid(r_h))
            o_ref[r0:r0 + c_len, vl] = o.astype(o_ref.dtype)

    xbuf_ref[SUBLANES:SUBLANES + ts, :] = z_ref[:, Z_X:Z_X + LRU_WIDTH]
    xc = xbuf_ref[SUBLANES - 3:SUBLANES - 3 + ts, :] * cw_ref[0:1, :]
    for tap in range(1, LRU_CONV_W):
        off = SUBLANES - (LRU_CONV_W - 1) + tap
        xc = xc + xbuf_ref[off:off + ts, :] * cw_ref[tap:tap + 1, :]
    xc = xc + cb_ref[...]
    xbuf_ref[0:SUBLANES, :] = xbuf_ref[ts:ts + SUBLANES, :]

    xcb = xc.astype(BF16)
    r_gate = _sigmoid(_dot(xcb, wa_ref[...]) + ba_ref[...])
    i_gate = _sigmoid(_dot(xcb, wx_ref[...]) + bx_ref[...])
    neg_lam = -lam_ref[...]
    softplus = jnp.maximum(neg_lam, 0.0) + jnp.log1p(jnp.exp(-jnp.abs(neg_lam)))
    log_a = -LRU_C * r_gate * softplus
    a_ref[...] = jnp.exp(log_a)
    th = jnp.tanh(log_a)
    u_ref[...] = jnp.sqrt(-2.0 * th / (1.0 - th)) * (i_gate * xc)

    srow = lax.broadcasted_iota(jnp.int32, (SUBLANES, LRU_WIDTH), 0)

    def scan_group(g, carry):
        r0 = pl.multiple_of(g * SUBLANES, SUBLANES)
        a = a_ref[pl.ds(r0, SUBLANES), :]
        u = u_ref[pl.ds(r0, SUBLANES), :]
        for s in (1, 2, 4):
            keep = srow >= s
            a_sh = jnp.where(keep, pltpu.roll(a, s, 0), 1.0)
            u_sh = jnp.where(keep, pltpu.roll(u, s, 0), 0.0)
            u = a * u_sh + u
            a = a * a_sh
        h = a * carry + u
        h_ref[pl.ds(r0, SUBLANES), :] = h
        return h[SUBLANES - 1:SUBLANES, :]

    hc_ref[...] = lax.fori_loop(0, ts // SUBLANES, scan_group, hc_ref[...], unroll=4)

    g_in = z_ref[:, Z_G:Z_G + LRU_WIDTH]
    gelu = 0.5 * g_in * (1.0 + jnp.tanh(math.sqrt(2.0 / math.pi) * (g_in + 0.044715 * (g_in * g_in * g_in))))
    o_ref[:, GLA_VW:GLA_VW + LRU_WIDTH] = (h_ref[...] * gelu).astype(o_ref.dtype)


def _hybrid_core(z, wf, bf, gn, cw, cb, wa, ba, wx, bx, lam):
    b, s, _ = z.shape
    const = lambda shape: pl.BlockSpec(shape, lambda i, j: (0,) * len(shape))
    return pl.pallas_call(
        _hybrid_body,
        grid=(b, s // HYB_TS),
        in_specs=[
            pl.BlockSpec((None, HYB_TS, HYB_ZW), lambda i, j: (i, j, 0)),
            const((LANES, HYB_QP)), const((1, HYB_QP)), const((1, GLA_VW)),
            const((LRU_CONV_W, LRU_WIDTH)), const((1, LRU_WIDTH)),
            const((LRU_WIDTH, LRU_WIDTH)), const((1, LRU_WIDTH)),
            const((LRU_WIDTH, LRU_WIDTH)), const((1, LRU_WIDTH)),
            const((1, LRU_WIDTH)),
        ],
        out_specs=pl.BlockSpec((None, HYB_TS, GLA_VW + LRU_WIDTH), lambda i, j: (i, j, 0)),
        out_shape=jax.ShapeDtypeStruct((b, s, GLA_VW + LRU_WIDTH), BF16),
        scratch_shapes=[
            pltpu.VMEM((GLA_HEADS, GLA_DV, LANES), F32),
            pltpu.VMEM((HYB_TS + 2 * SUBLANES, LRU_WIDTH), F32),
            pltpu.VMEM((HYB_TS, LRU_WIDTH), F32),
            pltpu.VMEM((HYB_TS, LRU_WIDTH), F32),
            pltpu.VMEM((HYB_TS, LRU_WIDTH), F32),
            pltpu.VMEM((1, LRU_WIDTH), F32),
        ],
        compiler_params=_params("parallel", "arbitrary"),
        name="hybrid_core",
    )(z, wf, bf, gn, cw, cb, wa, ba, wx, bx, lam)


def _pad_heads(w):
    rows = w.shape[0]
    w = w.reshape(rows, GLA_HEADS, GLA_DK)
    return jnp.pad(w, ((0, 0), (0, 0), (0, LANES - GLA_DK))).reshape(rows, GLA_HEADS * LANES)


def _block_diag(w):
    g, bw, _ = w.shape
    eye = jnp.eye(g, dtype=w.dtype)
    return (eye[:, None, :, None] * w[:, :, None, :]).reshape(g * bw, g * bw)


def _hybrid_mixer(x, norm_g, w_in, w_out, w_fup, b_f, gla_norm, conv_w, conv_b, w_a, b_a, w_x, b_x, lam, batch):
    t = x.shape[0]
    q_w, k_w, v_w, r_w, f_w, xr_w, gr_w = jnp.split(
        w_in, np.cumsum([GLA_KW, GLA_KW, GLA_VW, GLA_VW, GLA_GATE_RANK, LRU_WIDTH]).tolist(), axis=1)
    f_w = jnp.pad(f_w, ((0, 0), (0, LANES - GLA_GATE_RANK)))
    w_in_p = jnp.concatenate([_pad_heads(q_w), _pad_heads(k_w), v_w, r_w, xr_w, gr_w, f_w], axis=1).astype(BF16)
    z = _norm_proj(x, norm_g, w_in_p, jnp.zeros((HYB_ZW,), F32), F32, "hyb_in")
    wf = jnp.pad(_pad_heads(w_fup), ((0, LANES - GLA_GATE_RANK), (0, 0))).astype(BF16)
    mix = _hybrid_core(
        z.reshape(batch, t // batch, HYB_ZW), wf, _pad_heads(b_f.reshape(1, GLA_KW)), gla_norm.reshape(1, GLA_VW),
        conv_w, conv_b.reshape(1, LRU_WIDTH), _block_diag(w_a).astype(BF16), b_a.reshape(1, LRU_WIDTH),
        _block_diag(w_x).astype(BF16), b_x.reshape(1, LRU_WIDTH), lam.reshape(1, LRU_WIDTH))
    return _proj_res(mix.reshape(t, GLA_VW + LRU_WIDTH), w_out.astype(BF16), jnp.zeros((D_MODEL,), F32), x, "hyb_out")


def _t5_bucket_table():
    max_exact = REL_BUCKETS // 2
    dist = np.arange(SWA_BLOCK)
    d = np.maximum(dist, 1).astype(np.float32)
    large = max_exact + (np.log(d / max_exact) / math.log(REL_MAX_DIST / max_exact)
                         * (REL_BUCKETS - max_exact)).astype(np.int32)
    bucket = np.where(dist < max_exact, dist, np.minimum(large, REL_BUCKETS - 1)).astype(np.int32)
    i = np.arange(SWA_BLOCK)[:, None]
    j = np.arange(SWA_BLOCK)[None, :]
    return bucket[(i - j) % SWA_BLOCK]


def _swa_body(rel_ref, sink_ref, bkt_ref, q_ref, kc_ref, kp_ref, vc_ref, vp_ref, o_ref, bias_ref):
    blk = SWA_BLOCK
    dh = SWA_HEAD_DIM

    @pl.when((pl.program_id(0) == 0) & (pl.program_id(1) == 0))
    def _():
        bkt = bkt_ref[...]
        for h in range(SWA_HEADS):
            acc = jnp.zeros((blk, blk), F32)
            for k in range(REL_BUCKETS):
                acc = jnp.where(bkt == k, rel_ref[k, h], acc)
            bias_ref[h] = acc

    row = lax.broadcasted_iota(jnp.int32, (blk, blk), 0)
    col = lax.broadcasted_iota(jnp.int32, (blk, blk), 1)
    in_cur = col <= row
    has_prev = jnp.full((blk, blk), pl.program_id(1), jnp.int32) > 0
    valid = in_cur | has_prev
    for kv in range(SWA_KV_HEADS):
        kl = slice(kv * dh, (kv + 1) * dh)
        k_cur, k_prev, v_cur, v_prev = kc_ref[:, kl], kp_ref[:, kl], vc_ref[:, kl], vp_ref[:, kl]
        for g in range(SWA_GROUP):
            h = kv * SWA_GROUP + g
            q = q_ref[:, h * dh:(h + 1) * dh]
            s = jnp.where(in_cur, _dot_nt(q, k_cur), _dot_nt(q, k_prev)) * (dh ** -0.5) + bias_ref[h]
            s = jnp.where(valid, s, MASK_VALUE)
            sink = sink_ref[h]
            m = jnp.maximum(jnp.max(s, axis=-1, keepdims=True), sink)
            e = jnp.exp(s - m)
            p = e / (jnp.sum(e, axis=-1, keepdims=True) + jnp.exp(sink - m))
            o = _dot(jnp.where(in_cur, p, 0.0).astype(BF16), v_cur) + _dot(jnp.where(in_cur, 0.0, p).astype(BF16), v_prev)
            o_ref[:, h * dh:(h + 1) * dh] = o.astype(o_ref.dtype)


def _swa_core(z, rel_bias, sinks):
    b, s, _ = z.shape
    nb = s // SWA_BLOCK
    kcol = SWA_Q_W // SWA_KV_W
    vcol = kcol + 1
    smem = lambda: pl.BlockSpec(memory_space=pltpu.SMEM)
    return pl.pallas_call(
        _swa_body,
        grid=(b, nb),
        in_specs=[
            smem(), smem(),
            pl.BlockSpec((SWA_BLOCK, SWA_BLOCK), lambda i, j: (0, 0)),
            pl.BlockSpec((None, SWA_BLOCK, SWA_Q_W), lambda i, j: (i, j, 0)),
            pl.BlockSpec((None, SWA_BLOCK, SWA_KV_W), lambda i, j: (i, j, kcol)),
            pl.BlockSpec((None, SWA_BLOCK, SWA_KV_W), lambda i, j: (i, jnp.maximum(j - 1, 0), kcol)),
            pl.BlockSpec((None, SWA_BLOCK, SWA_KV_W), lambda i, j: (i, j, vcol)),
            pl.BlockSpec((None, SWA_BLOCK, SWA_KV_W), lambda i, j: (i, jnp.maximum(j - 1, 0), vcol)),
        ],
        out_specs=pl.BlockSpec((None, SWA_BLOCK, SWA_Q_W), lambda i, j: (i, j, 0)),
        out_shape=jax.ShapeDtypeStruct((b, s, SWA_Q_W), BF16),
        scratch_shapes=[pltpu.VMEM((SWA_HEADS, SWA_BLOCK, SWA_BLOCK), F32)],
        compiler_params=_params("arbitrary", "arbitrary"),
        name="swa_core",
    )(rel_bias, sinks, jnp.asarray(_t5_bucket_table()), z, z, z, z, z)


def _swa_mixer(x, norm_g, w_qkv, b_qkv, w_o, b_o, sinks, rel_bias, batch):
    t = x.shape[0]
    z = _norm_proj(x, norm_g, w_qkv.astype(BF16), b_qkv, BF16, "swa_qkv")
    o = _swa_core(z.reshape(batch, t // batch, SWA_Q_W + 2 * SWA_KV_W), rel_bias, sinks)
    return _proj_res(o.reshape(t, SWA_Q_W), w_o.astype(BF16), b_o, x, "swa_out")


def kernel(x, p, rel_bias, final_norm, ffn1_norm, ffn1_w_gate, ffn1_w_up, ffn1_w_down, mix_norm, ffn2_norm,
           ffn2_w_gate, ffn2_w_up, ffn2_w_down, ple_norm, ple_w_proj, ple_w_gate, hyb_w_in, hyb_w_out, gla_w_fup,
           gla_b_f, gla_norm, lru_conv_w, lru_conv_b, lru_w_a, lru_b_a, lru_w_x, lru_b_x, lru_lambda, swa_w_qkv,
           swa_b_qkv, swa_w_o, swa_b_o, swa_sinks):
    batch, seq, _ = x.shape
    t = batch * seq
    x = x.reshape(t, D_MODEL)
    p = p.reshape(DEPTH, t, PLE_DIM)
    for i in range(DEPTH):
        x = _ffn(x, ffn1_norm, ffn1_w_gate, ffn1_w_up, ffn1_w_down, i)
        if i % 2 == 0:
            e = i // 2
            x = _hybrid_mixer(x, mix_norm[i], hyb_w_in[e], hyb_w_out[e], gla_w_fup[e], gla_b_f[e], gla_norm[e],
                              lru_conv_w[e], lru_conv_b[e], lru_w_a[e], lru_b_a[e], lru_w_x[e], lru_b_x[e],
                              lru_lambda[e], batch)
        else:
            o = i // 2
            x = _swa_mixer(x, mix_norm[i], swa_w_qkv[o], swa_b_qkv[o], swa_w_o[o], swa_b_o[o], swa_sinks[o],
                           rel_bias, batch)
        x = _ffn(x, ffn2_norm, ffn2_w_gate, ffn2_w_up, ffn2_w_down, i)
        x = _ple(x, ple_norm[i], ple_w_gate[i].astype(BF16), p[i], ple_w_proj[i].astype(BF16), final_norm,
                 final=(i == DEPTH - 1))
    return x.reshape(batch, seq, D_MODEL)
```

```python
import functools
import math

import jax
import jax.numpy as jnp
import numpy as np
from jax import lax
from jax.experimental import pallas as pl
from jax.experimental.pallas import tpu as pltpu

F32 = jnp.float32
BF16 = jnp.bfloat16

D_MODEL = 1024
DEPTH = 4
PLE_DIM = 256
D_FF = 2816
EPS = 1e-6

GLA_HEADS = 4
GLA_DK = 64
GLA_DV = 128
GLA_KW = GLA_HEADS * GLA_DK
GLA_VW = GLA_HEADS * GLA_DV
GLA_GATE_RANK = 16
GLA_GATE_TAU = 16.0
GLA_CHUNK = 64
LRU_WIDTH = 512
LRU_BLOCKS = 8
LRU_CONV_W = 4
LRU_C = 8.0

SWA_HEADS = 16
SWA_KV_HEADS = 4
SWA_HEAD_DIM = 64
SWA_GROUP = SWA_HEADS // SWA_KV_HEADS
SWA_BLOCK = 128
SWA_Q_W = SWA_HEADS * SWA_HEAD_DIM
SWA_KV_W = SWA_KV_HEADS * SWA_HEAD_DIM
REL_BUCKETS = 32
REL_MAX_DIST = 128
MASK_VALUE = -1e30

LANES = 128
SUBLANES = 8
VMEM_LIMIT_BYTES = 56 * 1024 * 1024

FFN_TM = 1024
FFN_TF = 256
PROJ_TM = 512
HYB_TS = 256
SWA_TQ = 512

HYB_QP = GLA_HEADS * LANES
Z_Q = 0
Z_K = Z_Q + HYB_QP
Z_V = Z_K + HYB_QP
Z_R = Z_V + GLA_VW
Z_X = Z_R + GLA_VW
Z_G = Z_X + LRU_WIDTH
Z_F = Z_G + LRU_WIDTH
HYB_ZW = Z_F + LANES


def _params(*semantics):
    return pltpu.CompilerParams(dimension_semantics=semantics, vmem_limit_bytes=VMEM_LIMIT_BYTES)


def _rms(x, g):
    return x * lax.rsqrt(jnp.mean(x * x, axis=-1, keepdims=True) + EPS) * g


def _sigmoid(x):
    return 1.0 / (1.0 + jnp.exp(-x))


def _dot(a, b):
    return jnp.dot(a, b, preferred_element_type=F32)


def _dot_nt(a, b):
    return lax.dot_general(a, b, (((1,), (1,)), ((), ())), preferred_element_type=F32)


def _dot_tn(a, b):
    return lax.dot_general(a, b, (((0,), (0,)), ((), ())), preferred_element_type=F32)


def _ffn_body(x_ref, g_ref, wg_ref, wu_ref, wd_ref, o_ref, xn_ref):
    @pl.when(pl.program_id(1) == 0)
    def _():
        x = x_ref[...]
        xn_ref[...] = _rms(x, g_ref[...]).astype(BF16)
        o_ref[...] = x

    xn = xn_ref[...]
    a = _dot(xn, wg_ref[...].astype(BF16))
    b = _dot(xn, wu_ref[...].astype(BF16))
    h = (0.5 * a) * _sigmoid(a) * b
    o_ref[...] += _dot(h.astype(BF16), wd_ref[...].astype(BF16))


def _ffn(x, g, w_gate, w_up, w_down, layer):
    t = x.shape[0]
    return pl.pallas_call(
        _ffn_body,
        grid=(t // FFN_TM, D_FF // FFN_TF),
        in_specs=[
            pl.BlockSpec((FFN_TM, D_MODEL), lambda i, j: (i, 0)),
            pl.BlockSpec((None, 1, D_MODEL), lambda i, j: (layer, 0, 0)),
            pl.BlockSpec((None, D_MODEL, FFN_TF), lambda i, j: (layer, 0, j)),
            pl.BlockSpec((None, D_MODEL, FFN_TF), lambda i, j: (layer, 0, j)),
            pl.BlockSpec((None, FFN_TF, D_MODEL), lambda i, j: (layer, j, 0)),
        ],
        out_specs=pl.BlockSpec((FFN_TM, D_MODEL), lambda i, j: (i, 0)),
        out_shape=jax.ShapeDtypeStruct((t, D_MODEL), F32),
        scratch_shapes=[pltpu.VMEM((FFN_TM, D_MODEL), BF16)],
        compiler_params=_params("parallel", "arbitrary"),
        name="ffn",
    )(x, g.reshape(DEPTH, 1, D_MODEL), w_gate, w_up, w_down)


def _norm_proj_body(x_ref, g_ref, w_ref, b_ref, o_ref):
    xn = _rms(x_ref[...], g_ref[...]).astype(BF16)
    o_ref[...] = (_dot(xn, w_ref[...]) + b_ref[...]).astype(o_ref.dtype)


def _norm_proj(x, g, w, b, out_dtype, name):
    t = x.shape[0]
    n = w.shape[1]
    return pl.pallas_call(
        _norm_proj_body,
        grid=(t // PROJ_TM,),
        in_specs=[
            pl.BlockSpec((PROJ_TM, D_MODEL), lambda i: (i, 0)),
            pl.BlockSpec((1, D_MODEL), lambda i: (0, 0)),
            pl.BlockSpec((D_MODEL, n), lambda i: (0, 0)),
            pl.BlockSpec((1, n), lambda i: (0, 0)),
        ],
        out_specs=pl.BlockSpec((PROJ_TM, n), lambda i: (i, 0)),
        out_shape=jax.ShapeDtypeStruct((t, n), out_dtype),
        compiler_params=_params("parallel"),
        name=name,
    )(x, g.reshape(1, D_MODEL), w, b.reshape(1, n))


def _proj_res_body(a_ref, w_ref, b_ref, x_ref, o_ref):
    o_ref[...] = x_ref[...] + (_dot(a_ref[...], w_ref[...]) + b_ref[...])


def _proj_res(a, w, b, x, name):
    t, k = a.shape
    return pl.pallas_call(
        _proj_res_body,
        grid=(t // PROJ_TM,),
        in_specs=[
            pl.BlockSpec((PROJ_TM, k), lambda i: (i, 0)),
            pl.BlockSpec((k, D_MODEL), lambda i: (0, 0)),
            pl.BlockSpec((1, D_MODEL), lambda i: (0, 0)),
            pl.BlockSpec((PROJ_TM, D_MODEL), lambda i: (i, 0)),
        ],
        out_specs=pl.BlockSpec((PROJ_TM, D_MODEL), lambda i: (i, 0)),
        out_shape=jax.ShapeDtypeStruct((t, D_MODEL), F32),
        compiler_params=_params("parallel"),
        name=name,
    )(a, w, b.reshape(1, D_MODEL), x)


def _ple_body(x_ref, g_ref, wg_ref, p_ref, wp_ref, fg_ref, o_ref, *, final):
    x = x_ref[...]
    gate = _sigmoid(_dot(_rms(x, g_ref[...]).astype(BF16), wg_ref[...]))
    y = x + gate * _dot(p_ref[...].astype(BF16), wp_ref[...])
    if final:
        y = _rms(y, fg_ref[...])
    o_ref[...] = y


def _ple(x, g, w_gate, p, w_proj, final_g, final):
    t = x.shape[0]
    return pl.pallas_call(
        functools.partial(_ple_body, final=final),
        grid=(t // PROJ_TM,),
        in_specs=[
            pl.BlockSpec((PROJ_TM, D_MODEL), lambda i: (i, 0)),
            pl.BlockSpec((1, D_MODEL), lambda i: (0, 0)),
            pl.BlockSpec((D_MODEL, D_MODEL), lambda i: (0, 0)),
            pl.BlockSpec((PROJ_TM, PLE_DIM), lambda i: (i, 0)),
            pl.BlockSpec((PLE_DIM, D_MODEL), lambda i: (0, 0)),
            pl.BlockSpec((1, D_MODEL), lambda i: (0, 0)),
        ],
        out_specs=pl.BlockSpec((PROJ_TM, D_MODEL), lambda i: (i, 0)),
        out_shape=jax.ShapeDtypeStruct((t, D_MODEL), F32),
        compiler_params=_params("parallel"),
        name="ple",
    )(x, g.reshape(1, D_MODEL), w_gate, p, w_proj, final_g.reshape(1, D_MODEL))


def _split3(x):
    hi = x.astype(BF16)
    r1 = x - hi.astype(F32)
    mid = r1.astype(BF16)
    lo = (r1 - mid.astype(F32)).astype(BF16)
    return hi, mid, lo


def _hybrid_body(z_ref, wf_ref, bf_ref, gn_ref, cw_ref, cb_ref, wa_ref, ba_ref, wx_ref, bx_ref, lam_ref,
                 o_ref, st_ref, xbuf_ref, a_ref, u_ref, h_ref, hc_ref):
    ts = HYB_TS
    c_len = GLA_CHUNK

    @pl.when(pl.program_id(1) == 0)
    def _():
        st_ref[...] = jnp.zeros_like(st_ref)
        xbuf_ref[0:SUBLANES, :] = jnp.zeros((SUBLANES, LRU_WIDTH), F32)
        hc_ref[...] = jnp.zeros_like(hc_ref)

    f_low = z_ref[:, Z_F:Z_F + LANES].astype(BF16)
    gate_in = _dot(f_low, wf_ref[...]) + bf_ref[...]
    log_f = (jnp.minimum(gate_in, 0.0) - jnp.log1p(jnp.exp(-jnp.abs(gate_in)))) * (1.0 / GLA_GATE_TAU)
    row = lax.broadcasted_iota(jnp.int32, (ts, ts), 0)
    col = lax.broadcasted_iota(jnp.int32, (ts, ts), 1)
    tri = jnp.where((row // c_len == col // c_len) & (col <= row), 1.0, 0.0).astype(BF16)
    hi, mid, lo = _split3(log_f)
    b_all = _dot(tri, hi) + _dot(tri, mid) + _dot(tri, lo)

    crow = lax.broadcasted_iota(jnp.int32, (c_len, c_len), 0)
    ccol = lax.broadcasted_iota(jnp.int32, (c_len, c_len), 1)
    causal = ccol <= crow
    for c in range(ts // c_len):
        r0 = c * c_len
        b_c = b_all[r0:r0 + c_len]
        b_last = b_c[c_len - 1:c_len]
        q_c = z_ref[r0:r0 + c_len, Z_Q:Z_Q + HYB_QP]
        k_c = z_ref[r0:r0 + c_len, Z_K:Z_K + HYB_QP]
        q_dec = (q_c * (GLA_DK ** -0.5) * jnp.exp(b_c)).astype(BF16)
        k_dec = (k_c * jnp.exp(-b_c)).astype(BF16)
        k_end = (k_c * jnp.exp(b_last - b_c)).astype(BF16)
        decay = jnp.exp(b_last)
        v_c = z_ref[r0:r0 + c_len, Z_V:Z_V + GLA_VW].astype(BF16)
        r_c = z_ref[r0:r0 + c_len, Z_R:Z_R + GLA_VW]
        for hd in range(GLA_HEADS):
            kl = slice(hd * LANES, (hd + 1) * LANES)
            vl = slice(hd * GLA_DV, (hd + 1) * GLA_DV)
            att = jnp.where(causal, _dot_nt(q_dec[:, kl], k_dec[:, kl]), 0.0)
            s_t = st_ref[hd]
            o = _dot(att.astype(BF16), v_c[:, vl]) + _dot_nt(q_dec[:, kl], s_t.astype(BF16))
            st_ref[hd] = s_t * decay[:, kl] + _dot_tn(v_c[:, vl], k_end[:, kl])
            o = o * lax.rsqrt(jnp.mean(o * o, axis=-1, keepdims=True) + EPS)
            r_h = r_c[:, vl]
            o = o * gn_ref[:, vl] * (r_h * _sigmoid(r_h))
            o_ref[r0:r0 + c_len, vl] = o.astype(o_ref.dtype)

    xbuf_ref[SUBLANES:SUBLANES + ts, :] = z_ref[:, Z_X:Z_X + LRU_WIDTH]
    xc = xbuf_ref[SUBLANES - 3:SUBLANES - 3 + ts, :] * cw_ref[0:1, :]
    for tap in range(1, LRU_CONV_W):
        off = SUBLANES - (LRU_CONV_W - 1) + tap
        xc = xc + xbuf_ref[off:off + ts, :] * cw_ref[tap:tap + 1, :]
    xc = xc + cb_ref[...]
    xbuf_ref[0:SUBLANES, :] = xbuf_ref[ts:ts + SUBLANES, :]

    xcb = xc.astype(BF16)
    r_gate = _sigmoid(_dot(xcb, wa_ref[...]) + ba_ref[...])
    i_gate = _sigmoid(_dot(xcb, wx_ref[...]) + bx_ref[...])
    neg_lam = -lam_ref[...]
    softplus = jnp.maximum(neg_lam, 0.0) + jnp.log1p(jnp.exp(-jnp.abs(neg_lam)))
    log_a = -LRU_C * r_gate * softplus
    a_ref[...] = jnp.exp(log_a)
    th = jnp.tanh(log_a)
    u_ref[...] = jnp.sqrt(-2.0 * th / (1.0 - th)) * (i_gate * xc)

    srow = lax.broadcasted_iota(jnp.int32, (SUBLANES, LRU_WIDTH), 0)

    def scan_group(g, carry):
        r0 = pl.multiple_of(g * SUBLANES, SUBLANES)
        a = a_ref[pl.ds(r0, SUBLANES), :]
        u = u_ref[pl.ds(r0, SUBLANES), :]
        for s in (1, 2, 4):
            keep = srow >= s
            a_sh = jnp.where(keep, pltpu.roll(a, s, 0), 1.0)
            u_sh = jnp.where(keep, pltpu.roll(u, s, 0), 0.0)
            u = a * u_sh + u
            a = a * a_sh
        h = a * carry + u
        h_ref[pl.ds(r0, SUBLANES), :] = h
        return h[SUBLANES - 1:SUBLANES, :]

    hc_ref[...] = lax.fori_loop(0, ts // SUBLANES, scan_group, hc_ref[...], unroll=4)

    g_in = z_ref[:, Z_G:Z_G + LRU_WIDTH]
    gelu = 0.5 * g_in * (1.0 + jnp.tanh(math.sqrt(2.0 / math.pi) * (g_in + 0.044715 * (g_in * g_in * g_in))))
    o_ref[:, GLA_VW:GLA_VW + LRU_WIDTH] = (h_ref[...] * gelu).astype(o_ref.dtype)


def _hybrid_core(z, wf, bf, gn, cw, cb, wa, ba, wx, bx, lam):
    b, s, _ = z.shape
    const = lambda shape: pl.BlockSpec(shape, lambda i, j: (0,) * len(shape))
    return pl.pallas_call(
        _hybrid_body,
        grid=(b, s // HYB_TS),
        in_specs=[
            pl.BlockSpec((None, HYB_TS, HYB_ZW), lambda i, j: (i, j, 0)),
            const((LANES, HYB_QP)), const((1, HYB_QP)), const((1, GLA_VW)),
            const((LRU_CONV_W, LRU_WIDTH)), const((1, LRU_WIDTH)),
            const((LRU_WIDTH, LRU_WIDTH)), const((1, LRU_WIDTH)),
            const((LRU_WIDTH, LRU_WIDTH)), const((1, LRU_WIDTH)),
            const((1, LRU_WIDTH)),
        ],
        out_specs=pl.BlockSpec((None, HYB_TS, GLA_VW + LRU_WIDTH), lambda i, j: (i, j, 0)),
        out_shape=jax.ShapeDtypeStruct((b, s, GLA_VW + LRU_WIDTH), BF16),
        scratch_shapes=[
            pltpu.VMEM((GLA_HEADS, GLA_DV, LANES), F32),
            pltpu.VMEM((HYB_TS + 2 * SUBLANES, LRU_WIDTH), F32),
            pltpu.VMEM((HYB_TS, LRU_WIDTH), F32),
            pltpu.VMEM((HYB_TS, LRU_WIDTH), F32),
            pltpu.VMEM((HYB_TS, LRU_WIDTH), F32),
            pltpu.VMEM((1, LRU_WIDTH), F32),
        ],
        compiler_params=_params("parallel", "arbitrary"),
        name="hybrid_core",
    )(z, wf, bf, gn, cw, cb, wa, ba, wx, bx, lam)


def _pad_heads(w):
    rows = w.shape[0]
    w = w.reshape(rows, GLA_HEADS, GLA_DK)
    return jnp.pad(w, ((0, 0), (0, 0), (0, LANES - GLA_DK))).reshape(rows, GLA_HEADS * LANES)


def _block_diag(w):
    g, bw, _ = w.shape
    eye = jnp.eye(g, dtype=w.dtype)
    return (eye[:, None, :, None] * w[:, :, None, :]).reshape(g * bw, g * bw)


def _hybrid_mixer(x, norm_g, w_in, w_out, w_fup, b_f, gla_norm, conv_w, conv_b, w_a, b_a, w_x, b_x, lam, batch):
    t = x.shape[0]
    q_w, k_w, v_w, r_w, f_w, xr_w, gr_w = jnp.split(
        w_in, np.cumsum([GLA_KW, GLA_KW, GLA_VW, GLA_VW, GLA_GATE_RANK, LRU_WIDTH]).tolist(), axis=1)
    f_w = jnp.pad(f_w, ((0, 0), (0, LANES - GLA_GATE_RANK)))
    w_in_p = jnp.concatenate([_pad_heads(q_w), _pad_heads(k_w), v_w, r_w, xr_w, gr_w, f_w], axis=1).astype(BF16)
    z = _norm_proj(x, norm_g, w_in_p, jnp.zeros((HYB_ZW,), F32), F32, "hyb_in")
    wf = jnp.pad(_pad_heads(w_fup), ((0, LANES - GLA_GATE_RANK), (0, 0))).astype(BF16)
    mix = _hybrid_core(
        z.reshape(batch, t // batch, HYB_ZW), wf, _pad_heads(b_f.reshape(1, GLA_KW)), gla_norm.reshape(1, GLA_VW),
        conv_w, conv_b.reshape(1, LRU_WIDTH), _block_diag(w_a).astype(BF16), b_a.reshape(1, LRU_WIDTH),
        _block_diag(w_x).astype(BF16), b_x.reshape(1, LRU_WIDTH), lam.reshape(1, LRU_WIDTH))
    return _proj_res(mix.reshape(t, GLA_VW + LRU_WIDTH), w_out.astype(BF16), jnp.zeros((D_MODEL,), F32), x, "hyb_out")


def _t5_bucket_table():
    max_exact = REL_BUCKETS // 2
    dist = np.arange(SWA_BLOCK)
    d = np.maximum(dist, 1).astype(np.float32)
    large = max_exact + (np.log(d / max_exact) / math.log(REL_MAX_DIST / max_exact)
                         * (REL_BUCKETS - max_exact)).astype(np.int32)
    bucket = np.where(dist < max_exact, dist, np.minimum(large, REL_BUCKETS - 1)).astype(np.int32)
    i = np.arange(SWA_BLOCK)[:, None]
    j = np.arange(SWA_BLOCK)[None, :]
    return bucket[(i - j) % SWA_BLOCK]


def _roll_half_lanes(x):
    words = pltpu.bitcast(x, jnp.int32)
    return pltpu.bitcast(pltpu.roll(words, LANES // 2, 1), BF16)


def _swa_body(rel_ref, sink_ref, bkt_ref, q_ref, kvc_ref, kvp_ref, o_ref, bias_ref, kd_ref, vo_ref):
    blk = SWA_BLOCK
    dh = SWA_HEAD_DIM
    tq = SWA_TQ
    first_tile = pl.program_id(1) == 0
    row = lax.broadcasted_iota(jnp.int32, (blk, blk), 0)
    col = lax.broadcasted_iota(jnp.int32, (blk, blk), 1)
    in_cur = col <= row
    lo_half = col < dh

    @pl.when((pl.program_id(0) == 0) & first_tile)
    def _():
        bkt = bkt_ref[...]
        for h in range(SWA_HEADS):
            acc = jnp.zeros((blk, blk), F32)
            for k in range(REL_BUCKETS):
                acc = jnp.where(bkt == k, rel_ref[k, h], acc)
            bias_ref[0, h] = acc
            bias_ref[1, h] = jnp.where(in_cur, acc, MASK_VALUE)
        band_lane = lax.broadcasted_iota(jnp.int32, (tq + blk, LANES), 1)
        ones_lo = jnp.where(band_lane < dh, 1.0, 0.0).astype(BF16)
        for kv in range(SWA_KV_HEADS):
            vo_ref[kv, 0, :, LANES:] = ones_lo
            vo_ref[kv, 1, :, LANES:] = 1.0 - ones_lo

    def build_band(src_ref, r0, rows):
        lo = lax.broadcasted_iota(jnp.int32, (rows, LANES), 1) < dh
        zero = jnp.zeros((rows, LANES), BF16)
        for kv in range(SWA_KV_HEADS):
            c0 = LANES * (kv // 2)
            k_pair = src_ref[:, c0:c0 + LANES]
            v_pair = src_ref[:, SWA_KV_W + c0:SWA_KV_W + c0 + LANES]
            k_rot = _roll_half_lanes(k_pair)
            v_rot = _roll_half_lanes(v_pair)
            k_own, k_other, v_own, v_other = ((k_pair, k_rot, v_pair, v_rot) if kv % 2 == 0
                                              else (k_rot, k_pair, v_rot, v_pair))
            kd_ref[kv, r0:r0 + rows, :] = jnp.where(lo, k_own, k_other)
            vo_ref[kv, 0, r0:r0 + rows, :LANES] = jnp.where(lo, v_own, zero)
            vo_ref[kv, 1, r0:r0 + rows, :LANES] = jnp.where(lo, zero, v_other)

    build_band(kvp_ref, 0, blk)
    build_band(kvc_ref, blk, tq)

    lane = lax.broadcasted_iota(jnp.int32, (1, LANES), 1)
    q_scale = (jnp.where(lane < dh, dh ** -0.5, 0.0).astype(BF16), jnp.where(lane < dh, 0.0, dh ** -0.5).astype(BF16))
    zero_p = jnp.zeros((blk, blk), BF16)

    def sub_block(n, carry):
        r0 = pl.multiple_of(n * blk, blk)
        first = jnp.where(first_tile & (n == 0), 1, 0)
        scores = []
        for kv in range(SWA_KV_HEADS):
            k_band = kd_ref[kv, pl.ds(r0, 2 * blk), :]
            for pr in range(SWA_GROUP // 2):
                q_pair = q_ref[pl.ds(r0, blk), pl.ds(LANES * (2 * kv + pr), LANES)]
                for half in range(2):
                    h = SWA_GROUP * kv + 2 * pr + half
                    s = _dot_nt(q_pair * q_scale[half], k_band)
                    scores.append(jnp.where(in_cur, s[:, blk:], s[:, :blk]) + bias_ref[first, h])
        maxes = [jnp.maximum(jnp.max(s, axis=-1, keepdims=True), sink_ref[h]) for h, s in enumerate(scores)]
        exps = [jnp.exp(s - m).astype(BF16) for s, m in zip(scores, maxes)]
        for kv in range(SWA_KV_HEADS):
            for pr in range(SWA_GROUP // 2):
                h0 = SWA_GROUP * kv + 2 * pr
                acc = None
                for half in range(2):
                    e = exps[h0 + half]
                    p_band = jnp.concatenate([jnp.where(in_cur, zero_p, e), jnp.where(in_cur, e, zero_p)], axis=1)
                    part = _dot(p_band, vo_ref[kv, half, pl.ds(r0, 2 * blk), :])
                    acc = part if acc is None else acc + part
                sink_term = jnp.where(lo_half, jnp.exp(sink_ref[h0] - maxes[h0]),
                                      jnp.exp(sink_ref[h0 + 1] - maxes[h0 + 1]))
                out = acc[:, :LANES] / (acc[:, LANES:] + sink_term)
                o_ref[pl.ds(r0, blk), pl.ds(LANES * (2 * kv + pr), LANES)] = out.astype(o_ref.dtype)
        return carry

    lax.fori_loop(0, tq // blk, sub_block, 0)


def _swa_core(z, rel_bias, sinks):
    b, s, _ = z.shape
    kv_col = SWA_Q_W // (2 * SWA_KV_W)
    blocks_per_tile = SWA_TQ // SWA_BLOCK
    smem = lambda: pl.BlockSpec(memory_space=pltpu.SMEM)
    return pl.pallas_call(
        _swa_body,
        grid=(b, s // SWA_TQ),
        in_specs=[
            smem(), smem(),
            pl.BlockSpec((SWA_BLOCK, SWA_BLOCK), lambda i, j: (0, 0)),
            pl.BlockSpec((None, SWA_TQ, SWA_Q_W), lambda i, j: (i, j, 0)),
            pl.BlockSpec((None, SWA_TQ, 2 * SWA_KV_W), lambda i, j: (i, j, kv_col)),
            pl.BlockSpec((None, SWA_BLOCK, 2 * SWA_KV_W),
                         lambda i, j: (i, jnp.maximum(j * blocks_per_tile - 1, 0), kv_col)),
        ],
        out_specs=pl.BlockSpec((None, SWA_TQ, SWA_Q_W), lambda i, j: (i, j, 0)),
        out_shape=jax.ShapeDtypeStruct((b, s, SWA_Q_W), BF16),
        scratch_shapes=[
            pltpu.VMEM((2, SWA_HEADS, SWA_BLOCK, SWA_BLOCK), F32),
            pltpu.VMEM((SWA_KV_HEADS, SWA_TQ + SWA_BLOCK, LANES), BF16),
            pltpu.VMEM((SWA_KV_HEADS, 2, SWA_TQ + SWA_BLOCK, 2 * LANES), BF16),
        ],
        compiler_params=_params("arbitrary", "arbitrary"),
        name="swa_core",
    )(rel_bias, sinks, jnp.asarray(_t5_bucket_table()), z, z, z)


def _swa_mixer(x, norm_g, w_qkv, b_qkv, w_o, b_o, sinks, rel_bias, batch):
    t = x.shape[0]
    z = _norm_proj(x, norm_g, w_qkv.astype(BF16), b_qkv, BF16, "swa_qkv")
    o = _swa_core(z.reshape(batch, t // batch, SWA_Q_W + 2 * SWA_KV_W), rel_bias, sinks)
    return _proj_res(o.reshape(t, SWA_Q_W), w_o.astype(BF16), b_o, x, "swa_out")


def kernel(x, p, rel_bias, final_norm, ffn1_norm, ffn1_w_gate, ffn1_w_up, ffn1_w_down, mix_norm, ffn2_norm,
           ffn2_w_gate, ffn2_w_up, ffn2_w_down, ple_norm, ple_w_proj, ple_w_gate, hyb_w_in, hyb_w_out, gla_w_fup,
           gla_b_f, gla_norm, lru_conv_w, lru_conv_b, lru_w_a, lru_b_a, lru_w_x, lru_b_x, lru_lambda, swa_w_qkv,
           swa_b_qkv, swa_w_o, swa_b_o, swa_sinks):
    batch, seq, _ = x.shape
    t = batch * seq
    x = x.reshape(t, D_MODEL)
    p = p.reshape(DEPTH, t, PLE_DIM)
    for i in range(DEPTH):
        x = _ffn(x, ffn1_norm, ffn1_w_gate, ffn1_w_up, ffn1_w_down, i)
        if i % 2 == 0:
            e = i // 2
            x = _hybrid_mixer(x, mix_norm[i], hyb_w_in[e], hyb_w_out[e], gla_w_fup[e], gla_b_f[e], gla_norm[e],
                              lru_conv_w[e], lru_conv_b[e], lru_w_a[e], lru_b_a[e], lru_w_x[e], lru_b_x[e],
                              lru_lambda[e], batch)
        else:
            o = i // 2
            x = _swa_mixer(x, mix_norm[i], swa_w_qkv[o], swa_b_qkv[o], swa_w_o[o], swa_b_o[o], swa_sinks[o],
                           rel_bias, batch)
        x = _ffn(x, ffn2_norm, ffn2_w_gate, ffn2_w_up, ffn2_w_down, i)
        x = _ple(x, ple_norm[i], ple_w_gate[i].astype(BF16), p[i], ple_w_proj[i].astype(BF16), final_norm,
                 final=(i == DEPTH - 1))
    return x.reshape(batch, seq, D_MODEL)
```

```python
import functools
import math

import jax
import jax.numpy as jnp
import numpy as np
from jax import lax
from jax.experimental import pallas as pl
from jax.experimental.pallas import tpu as pltpu

F32 = jnp.float32
BF16 = jnp.bfloat16

D_MODEL = 1024
DEPTH = 4
PLE_DIM = 256
D_FF = 2816
EPS = 1e-6

GLA_HEADS = 4
GLA_DK = 64
GLA_DV = 128
GLA_KW = GLA_HEADS * GLA_DK
GLA_VW = GLA_HEADS * GLA_DV
GLA_GATE_RANK = 16
GLA_GATE_TAU = 16.0
GLA_CHUNK = 64
LRU_WIDTH = 512
LRU_BLOCKS = 8
LRU_CONV_W = 4
LRU_C = 8.0

SWA_HEADS = 16
SWA_KV_HEADS = 4
SWA_HEAD_DIM = 64
SWA_GROUP = SWA_HEADS // SWA_KV_HEADS
SWA_BLOCK = 128
SWA_Q_W = SWA_HEADS * SWA_HEAD_DIM
SWA_KV_W = SWA_KV_HEADS * SWA_HEAD_DIM
REL_BUCKETS = 32
REL_MAX_DIST = 128
MASK_VALUE = -1e30

LANES = 128
SUBLANES = 8
VMEM_LIMIT_BYTES = 56 * 1024 * 1024

FFN_TM = 512
FFN_TF = 256
PROJ_TM = 512
HYB_TS = 256
SWA_TQ = 512

HYB_QP = GLA_HEADS * LANES
Z_Q = 0
Z_K = Z_Q + HYB_QP
Z_V = Z_K + HYB_QP
Z_R = Z_V + GLA_VW
Z_X = Z_R + GLA_VW
Z_G = Z_X + LRU_WIDTH
Z_F = Z_G + LRU_WIDTH
HYB_ZW = Z_F + LANES


def _params(*semantics):
    return pltpu.CompilerParams(dimension_semantics=semantics, vmem_limit_bytes=VMEM_LIMIT_BYTES)


def _rms(x, g):
    return x * lax.rsqrt(jnp.mean(x * x, axis=-1, keepdims=True) + EPS) * g


def _sigmoid(x):
    return 1.0 / (1.0 + jnp.exp(-x))


def _dot(a, b):
    return jnp.dot(a, b, preferred_element_type=F32)


def _dot_nt(a, b):
    return lax.dot_general(a, b, (((1,), (1,)), ((), ())), preferred_element_type=F32)


def _dot_tn(a, b):
    return lax.dot_general(a, b, (((0,), (0,)), ((), ())), preferred_element_type=F32)


def _ffn_body(x_ref, g_ref, wg_ref, wu_ref, wd_ref, o_ref):
    x = x_ref[...]
    xn = _rms(x, g_ref[...]).astype(BF16)
    acc = x
    for c0 in range(0, D_FF, FFN_TF):
        a = _dot(xn, wg_ref[:, c0:c0 + FFN_TF])
        b = _dot(xn, wu_ref[:, c0:c0 + FFN_TF])
        h = (0.5 * a) * _sigmoid(a) * b
        acc = acc + _dot(h.astype(BF16), wd_ref[c0:c0 + FFN_TF, :])
    o_ref[...] = acc


def _ffn(x, g, w_gate, w_up, w_down, layer):
    t = x.shape[0]
    resident = functools.partial(pl.BlockSpec, pipeline_mode=pl.Buffered(1))
    return pl.pallas_call(
        _ffn_body,
        grid=(t // FFN_TM,),
        in_specs=[
            pl.BlockSpec((FFN_TM, D_MODEL), lambda i: (i, 0)),
            pl.BlockSpec((None, 1, D_MODEL), lambda i: (layer, 0, 0)),
            resident((None, D_MODEL, D_FF), lambda i: (layer, 0, 0)),
            resident((None, D_MODEL, D_FF), lambda i: (layer, 0, 0)),
            resident((None, D_FF, D_MODEL), lambda i: (layer, 0, 0)),
        ],
        out_specs=pl.BlockSpec((FFN_TM, D_MODEL), lambda i: (i, 0)),
        out_shape=jax.ShapeDtypeStruct((t, D_MODEL), F32),
        compiler_params=_params("parallel"),
        name="ffn",
    )(x, g.reshape(DEPTH, 1, D_MODEL), w_gate, w_up, w_down)


def _norm_proj_body(x_ref, g_ref, w_ref, b_ref, o_ref):
    xn = _rms(x_ref[...], g_ref[...]).astype(BF16)
    o_ref[...] = (_dot(xn, w_ref[...]) + b_ref[...]).astype(o_ref.dtype)


def _norm_proj(x, g, w, b, out_dtype, name):
    t = x.shape[0]
    n = w.shape[1]
    return pl.pallas_call(
        _norm_proj_body,
        grid=(t // PROJ_TM,),
        in_specs=[
            pl.BlockSpec((PROJ_TM, D_MODEL), lambda i: (i, 0)),
            pl.BlockSpec((1, D_MODEL), lambda i: (0, 0)),
            pl.BlockSpec((D_MODEL, n), lambda i: (0, 0)),
            pl.BlockSpec((1, n), lambda i: (0, 0)),
        ],
        out_specs=pl.BlockSpec((PROJ_TM, n), lambda i: (i, 0)),
        out_shape=jax.ShapeDtypeStruct((t, n), out_dtype),
        compiler_params=_params("parallel"),
        name=name,
    )(x, g.reshape(1, D_MODEL), w, b.reshape(1, n))


def _proj_res_body(a_ref, w_ref, b_ref, x_ref, o_ref):
    o_ref[...] = x_ref[...] + (_dot(a_ref[...], w_ref[...]) + b_ref[...])


def _proj_res(a, w, b, x, name):
    t, k = a.shape
    return pl.pallas_call(
        _proj_res_body,
        grid=(t // PROJ_TM,),
        in_specs=[
            pl.BlockSpec((PROJ_TM, k), lambda i: (i, 0)),
            pl.BlockSpec((k, D_MODEL), lambda i: (0, 0)),
            pl.BlockSpec((1, D_MODEL), lambda i: (0, 0)),
            pl.BlockSpec((PROJ_TM, D_MODEL), lambda i: (i, 0)),
        ],
        out_specs=pl.BlockSpec((PROJ_TM, D_MODEL), lambda i: (i, 0)),
        out_shape=jax.ShapeDtypeStruct((t, D_MODEL), F32),
        compiler_params=_params("parallel"),
        name=name,
    )(a, w, b.reshape(1, D_MODEL), x)


def _ple_body(x_ref, g_ref, wg_ref, p_ref, wp_ref, fg_ref, o_ref, *, final):
    x = x_ref[...]
    gate = _sigmoid(_dot(_rms(x, g_ref[...]).astype(BF16), wg_ref[...]))
    y = x + gate * _dot(p_ref[...].astype(BF16), wp_ref[...])
    if final:
        y = _rms(y, fg_ref[...])
    o_ref[...] = y


def _ple(x, g, w_gate, p, w_proj, final_g, final):
    t = x.shape[0]
    return pl.pallas_call(
        functools.partial(_ple_body, final=final),
        grid=(t // PROJ_TM,),
        in_specs=[
            pl.BlockSpec((PROJ_TM, D_MODEL), lambda i: (i, 0)),
            pl.BlockSpec((1, D_MODEL), lambda i: (0, 0)),
            pl.BlockSpec((D_MODEL, D_MODEL), lambda i: (0, 0)),
            pl.BlockSpec((PROJ_TM, PLE_DIM), lambda i: (i, 0)),
            pl.BlockSpec((PLE_DIM, D_MODEL), lambda i: (0, 0)),
            pl.BlockSpec((1, D_MODEL), lambda i: (0, 0)),
        ],
        out_specs=pl.BlockSpec((PROJ_TM, D_MODEL), lambda i: (i, 0)),
        out_shape=jax.ShapeDtypeStruct((t, D_MODEL), F32),
        compiler_params=_params("parallel"),
        name="ple",
    )(x, g.reshape(1, D_MODEL), w_gate, p, w_proj, final_g.reshape(1, D_MODEL))


def _split3(x):
    hi = x.astype(BF16)
    r1 = x - hi.astype(F32)
    mid = r1.astype(BF16)
    lo = (r1 - mid.astype(F32)).astype(BF16)
    return hi, mid, lo


def _hybrid_body(z_ref, wf_ref, bf_ref, gn_ref, cw_ref, cb_ref, wa_ref, ba_ref, wx_ref, bx_ref, lam_ref,
                 o_ref, st_ref, xbuf_ref, a_ref, u_ref, h_ref, hc_ref):
    ts = HYB_TS
    c_len = GLA_CHUNK

    @pl.when(pl.program_id(1) == 0)
    def _():
        st_ref[...] = jnp.zeros_like(st_ref)
        xbuf_ref[0:SUBLANES, :] = jnp.zeros((SUBLANES, LRU_WIDTH), F32)
        hc_ref[...] = jnp.zeros_like(hc_ref)

    f_low = z_ref[:, Z_F:Z_F + LANES].astype(BF16)
    gate_in = _dot(f_low, wf_ref[...]) + bf_ref[...]
    log_f = (jnp.minimum(gate_in, 0.0) - jnp.log1p(jnp.exp(-jnp.abs(gate_in)))) * (1.0 / GLA_GATE_TAU)
    row = lax.broadcasted_iota(jnp.int32, (ts, ts), 0)
    col = lax.broadcasted_iota(jnp.int32, (ts, ts), 1)
    tri = jnp.where((row // c_len == col // c_len) & (col <= row), 1.0, 0.0).astype(BF16)
    hi, mid, lo = _split3(log_f)
    b_all = _dot(tri, hi) + _dot(tri, mid) + _dot(tri, lo)

    crow = lax.broadcasted_iota(jnp.int32, (c_len, c_len), 0)
    ccol = lax.broadcasted_iota(jnp.int32, (c_len, c_len), 1)
    causal = ccol <= crow
    for c in range(ts // c_len):
        r0 = c * c_len
        b_c = b_all[r0:r0 + c_len]
        b_last = b_c[c_len - 1:c_len]
        q_c = z_ref[r0:r0 + c_len, Z_Q:Z_Q + HYB_QP]
        k_c = z_ref[r0:r0 + c_len, Z_K:Z_K + HYB_QP]
        q_dec = (q_c * (GLA_DK ** -0.5) * jnp.exp(b_c)).astype(BF16)
        k_dec = (k_c * jnp.exp(-b_c)).astype(BF16)
        k_end = (k_c * jnp.exp(b_last - b_c)).astype(BF16)
        decay = jnp.exp(b_last)
        v_c = z_ref[r0:r0 + c_len, Z_V:Z_V + GLA_VW].astype(BF16)
        r_c = z_ref[r0:r0 + c_len, Z_R:Z_R + GLA_VW]
        for hd in range(GLA_HEADS):
            kl = slice(hd * LANES, (hd + 1) * LANES)
            vl = slice(hd * GLA_DV, (hd + 1) * GLA_DV)
            att = jnp.where(causal, _dot_nt(q_dec[:, kl], k_dec[:, kl]), 0.0)
            s_t = st_ref[hd]
            o = _dot(att.astype(BF16), v_c[:, vl]) + _dot_nt(q_dec[:, kl], s_t.astype(BF16))
            st_ref[hd] = s_t * decay[:, kl] + _dot_tn(v_c[:, vl], k_end[:, kl])
            o = o * lax.rsqrt(jnp.mean(o * o, axis=-1, keepdims=True) + EPS)
            r_h = r_c[:, vl]
            o = o * gn_ref[:, vl] * (r_h * _sigmoid(r_h))
            o_ref[r0:r0 + c_len, vl] = o.astype(o_ref.dtype)

    xbuf_ref[SUBLANES:SUBLANES + ts, :] = z_ref[:, Z_X:Z_X + LRU_WIDTH]
    xc = xbuf_ref[SUBLANES - 3:SUBLANES - 3 + ts, :] * cw_ref[0:1, :]
    for tap in range(1, LRU_CONV_W):
        off = SUBLANES - (LRU_CONV_W - 1) + tap
        xc = xc + xbuf_ref[off:off + ts, :] * cw_ref[tap:tap + 1, :]
    xc = xc + cb_ref[...]
    xbuf_ref[0:SUBLANES, :] = xbuf_ref[ts:ts + SUBLANES, :]

    xcb = xc.astype(BF16)
    r_gate = _sigmoid(_dot(xcb, wa_ref[...]) + ba_ref[...])
    i_gate = _sigmoid(_dot(xcb, wx_ref[...]) + bx_ref[...])
    neg_lam = -lam_ref[...]
    softplus = jnp.maximum(neg_lam, 0.0) + jnp.log1p(jnp.exp(-jnp.abs(neg_lam)))
    log_a = -LRU_C * r_gate * softplus
    a_ref[...] = jnp.exp(log_a)
    th = jnp.tanh(log_a)
    u_ref[...] = jnp.sqrt(-2.0 * th / (1.0 - th)) * (i_gate * xc)

    srow = lax.broadcasted_iota(jnp.int32, (SUBLANES, LRU_WIDTH), 0)

    def scan_group(g, carry):
        r0 = pl.multiple_of(g * SUBLANES, SUBLANES)
        a = a_ref[pl.ds(r0, SUBLANES), :]
        u = u_ref[pl.ds(r0, SUBLANES), :]
        for s in (1, 2, 4):
            keep = srow >= s
            a_sh = jnp.where(keep, pltpu.roll(a, s, 0), 1.0)
            u_sh = jnp.where(keep, pltpu.roll(u, s, 0), 0.0)
            u = a * u_sh + u
            a = a * a_sh
        h = a * carry + u
        h_ref[pl.ds(r0, SUBLANES), :] = h
        return h[SUBLANES - 1:SUBLANES, :]

    hc_ref[...] = lax.fori_loop(0, ts // SUBLANES, scan_group, hc_ref[...], unroll=4)

    g_in = z_ref[:, Z_G:Z_G + LRU_WIDTH]
    gelu = 0.5 * g_in * (1.0 + jnp.tanh(math.sqrt(2.0 / math.pi) * (g_in + 0.044715 * (g_in * g_in * g_in))))
    o_ref[:, GLA_VW:GLA_VW + LRU_WIDTH] = (h_ref[...] * gelu).astype(o_ref.dtype)


def _hybrid_core(z, wf, bf, gn, cw, cb, wa, ba, wx, bx, lam):
    b, s, _ = z.shape
    const = lambda shape: pl.BlockSpec(shape, lambda i, j: (0,) * len(shape))
    return pl.pallas_call(
        _hybrid_body,
        grid=(b, s // HYB_TS),
        in_specs=[
            pl.BlockSpec((None, HYB_TS, HYB_ZW), lambda i, j: (i, j, 0)),
            const((LANES, HYB_QP)), const((1, HYB_QP)), const((1, GLA_VW)),
            const((LRU_CONV_W, LRU_WIDTH)), const((1, LRU_WIDTH)),
            const((LRU_WIDTH, LRU_WIDTH)), const((1, LRU_WIDTH)),
            const((LRU_WIDTH, LRU_WIDTH)), const((1, LRU_WIDTH)),
            const((1, LRU_WIDTH)),
        ],
        out_specs=pl.BlockSpec((None, HYB_TS, GLA_VW + LRU_WIDTH), lambda i, j: (i, j, 0)),
        out_shape=jax.ShapeDtypeStruct((b, s, GLA_VW + LRU_WIDTH), BF16),
        scratch_shapes=[
            pltpu.VMEM((GLA_HEADS, GLA_DV, LANES), F32),
            pltpu.VMEM((HYB_TS + 2 * SUBLANES, LRU_WIDTH), F32),
            pltpu.VMEM((HYB_TS, LRU_WIDTH), F32),
            pltpu.VMEM((HYB_TS, LRU_WIDTH), F32),
            pltpu.VMEM((HYB_TS, LRU_WIDTH), F32),
            pltpu.VMEM((1, LRU_WIDTH), F32),
        ],
        compiler_params=_params("parallel", "arbitrary"),
        name="hybrid_core",
    )(z, wf, bf, gn, cw, cb, wa, ba, wx, bx, lam)


def _pad_heads(w):
    rows = w.shape[0]
    w = w.reshape(rows, GLA_HEADS, GLA_DK)
    return jnp.pad(w, ((0, 0), (0, 0), (0, LANES - GLA_DK))).reshape(rows, GLA_HEADS * LANES)


def _block_diag(w):
    g, bw, _ = w.shape
    eye = jnp.eye(g, dtype=w.dtype)
    return (eye[:, None, :, None] * w[:, :, None, :]).reshape(g * bw, g * bw)


def _hybrid_mixer(x, norm_g, w_in, w_out, w_fup, b_f, gla_norm, conv_w, conv_b, w_a, b_a, w_x, b_x, lam, batch):
    t = x.shape[0]
    q_w, k_w, v_w, r_w, f_w, xr_w, gr_w = jnp.split(
        w_in, np.cumsum([GLA_KW, GLA_KW, GLA_VW, GLA_VW, GLA_GATE_RANK, LRU_WIDTH]).tolist(), axis=1)
    f_w = jnp.pad(f_w, ((0, 0), (0, LANES - GLA_GATE_RANK)))
    w_in_p = jnp.concatenate([_pad_heads(q_w), _pad_heads(k_w), v_w, r_w, xr_w, gr_w, f_w], axis=1).astype(BF16)
    z = _norm_proj(x, norm_g, w_in_p, jnp.zeros((HYB_ZW,), F32), F32, "hyb_in")
    wf = jnp.pad(_pad_heads(w_fup), ((0, LANES - GLA_GATE_RANK), (0, 0))).astype(BF16)
    mix = _hybrid_core(
        z.reshape(batch, t // batch, HYB_ZW), wf, _pad_heads(b_f.reshape(1, GLA_KW)), gla_norm.reshape(1, GLA_VW),
        conv_w, conv_b.reshape(1, LRU_WIDTH), _block_diag(w_a).astype(BF16), b_a.reshape(1, LRU_WIDTH),
        _block_diag(w_x).astype(BF16), b_x.reshape(1, LRU_WIDTH), lam.reshape(1, LRU_WIDTH))
    return _proj_res(mix.reshape(t, GLA_VW + LRU_WIDTH), w_out.astype(BF16), jnp.zeros((D_MODEL,), F32), x, "hyb_out")


def _t5_bucket_table():
    max_exact = REL_BUCKETS // 2
    dist = np.arange(SWA_BLOCK)
    d = np.maximum(dist, 1).astype(np.float32)
    large = max_exact + (np.log(d / max_exact) / math.log(REL_MAX_DIST / max_exact)
                         * (REL_BUCKETS - max_exact)).astype(np.int32)
    bucket = np.where(dist < max_exact, dist, np.minimum(large, REL_BUCKETS - 1)).astype(np.int32)
    i = np.arange(SWA_BLOCK)[:, None]
    j = np.arange(SWA_BLOCK)[None, :]
    return bucket[(i - j) % SWA_BLOCK]


def _roll_half_lanes(x):
    words = pltpu.bitcast(x, jnp.int32)
    return pltpu.bitcast(pltpu.roll(words, LANES // 2, 1), BF16)


def _swa_body(rel_ref, sink_ref, bkt_ref, q_ref, kvc_ref, kvp_ref, o_ref, bias_ref, kd_ref, vo_ref):
    blk = SWA_BLOCK
    dh = SWA_HEAD_DIM
    tq = SWA_TQ
    first_tile = pl.program_id(1) == 0
    row = lax.broadcasted_iota(jnp.int32, (blk, blk), 0)
    col = lax.broadcasted_iota(jnp.int32, (blk, blk), 1)
    in_cur = col <= row
    lo_half = col < dh

    @pl.when((pl.program_id(0) == 0) & first_tile)
    def _():
        bkt = bkt_ref[...]
        for h in range(SWA_HEADS):
            acc = jnp.zeros((blk, blk), F32)
            for k in range(REL_BUCKETS):
                acc = jnp.where(bkt == k, rel_ref[k, h], acc)
            bias_ref[0, h] = acc
            bias_ref[1, h] = jnp.where(in_cur, acc, MASK_VALUE)
        band_lane = lax.broadcasted_iota(jnp.int32, (tq + blk, LANES), 1)
        ones_lo = jnp.where(band_lane < dh, 1.0, 0.0).astype(BF16)
        for kv in range(SWA_KV_HEADS):
            vo_ref[kv, 0, :, LANES:] = ones_lo
            vo_ref[kv, 1, :, LANES:] = 1.0 - ones_lo

    def build_band(src_ref, r0, rows):
        lo = lax.broadcasted_iota(jnp.int32, (rows, LANES), 1) < dh
        zero = jnp.zeros((rows, LANES), BF16)
        for kv in range(SWA_KV_HEADS):
            c0 = LANES * (kv // 2)
            k_pair = src_ref[:, c0:c0 + LANES]
            v_pair = src_ref[:, SWA_KV_W + c0:SWA_KV_W + c0 + LANES]
            k_rot = _roll_half_lanes(k_pair)
            v_rot = _roll_half_lanes(v_pair)
            k_own, k_other, v_own, v_other = ((k_pair, k_rot, v_pair, v_rot) if kv % 2 == 0
                                              else (k_rot, k_pair, v_rot, v_pair))
            kd_ref[kv, r0:r0 + rows, :] = jnp.where(lo, k_own, k_other)
            vo_ref[kv, 0, r0:r0 + rows, :LANES] = jnp.where(lo, v_own, zero)
            vo_ref[kv, 1, r0:r0 + rows, :LANES] = jnp.where(lo, zero, v_other)

    build_band(kvp_ref, 0, blk)
    build_band(kvc_ref, blk, tq)

    lane = lax.broadcasted_iota(jnp.int32, (1, LANES), 1)
    q_scale = (jnp.where(lane < dh, dh ** -0.5, 0.0).astype(BF16), jnp.where(lane < dh, 0.0, dh ** -0.5).astype(BF16))
    zero_p = jnp.zeros((blk, blk), BF16)

    def sub_block(n, carry):
        r0 = pl.multiple_of(n * blk, blk)
        first = jnp.where(first_tile & (n == 0), 1, 0)
        scores = []
        for kv in range(SWA_KV_HEADS):
            k_band = kd_ref[kv, pl.ds(r0, 2 * blk), :]
            for pr in range(SWA_GROUP // 2):
                q_pair = q_ref[pl.ds(r0, blk), pl.ds(LANES * (2 * kv + pr), LANES)]
                for half in range(2):
                    h = SWA_GROUP * kv + 2 * pr + half
                    s = _dot_nt(q_pair * q_scale[half], k_band)
                    scores.append(jnp.where(in_cur, s[:, blk:], s[:, :blk]) + bias_ref[first, h])
        maxes = [jnp.maximum(jnp.max(s, axis=-1, keepdims=True), sink_ref[h]) for h, s in enumerate(scores)]
        exps = [jnp.exp(s - m).astype(BF16) for s, m in zip(scores, maxes)]
        for kv in range(SWA_KV_HEADS):
            for pr in range(SWA_GROUP // 2):
                h0 = SWA_GROUP * kv + 2 * pr
                acc = None
                for half in range(2):
                    e = exps[h0 + half]
                    p_band = jnp.concatenate([jnp.where(in_cur, zero_p, e), jnp.where(in_cur, e, zero_p)], axis=1)
                    part = _dot(p_band, vo_ref[kv, half, pl.ds(r0, 2 * blk), :])
                    acc = part if acc is None else acc + part
                sink_term = jnp.where(lo_half, jnp.exp(sink_ref[h0] - maxes[h0]),
                                      jnp.exp(sink_ref[h0 + 1] - maxes[h0 + 1]))
                out = acc[:, :LANES] / (acc[:, LANES:] + sink_term)
                o_ref[pl.ds(r0, blk), pl.ds(LANES * (2 * kv + pr), LANES)] = out.astype(o_ref.dtype)
        return carry

    lax.fori_loop(0, tq // blk, sub_block, 0)


def _swa_core(z, rel_bias, sinks):
    b, s, _ = z.shape
    kv_col = SWA_Q_W // (2 * SWA_KV_W)
    blocks_per_tile = SWA_TQ // SWA_BLOCK
    smem = lambda: pl.BlockSpec(memory_space=pltpu.SMEM)
    return pl.pallas_call(
        _swa_body,
        grid=(b, s // SWA_TQ),
        in_specs=[
            smem(), smem(),
            pl.BlockSpec((SWA_BLOCK, SWA_BLOCK), lambda i, j: (0, 0)),
            pl.BlockSpec((None, SWA_TQ, SWA_Q_W), lambda i, j: (i, j, 0)),
            pl.BlockSpec((None, SWA_TQ, 2 * SWA_KV_W), lambda i, j: (i, j, kv_col)),
            pl.BlockSpec((None, SWA_BLOCK, 2 * SWA_KV_W),
                         lambda i, j: (i, jnp.maximum(j * blocks_per_tile - 1, 0), kv_col)),
        ],
        out_specs=pl.BlockSpec((None, SWA_TQ, SWA_Q_W), lambda i, j: (i, j, 0)),
        out_shape=jax.ShapeDtypeStruct((b, s, SWA_Q_W), BF16),
        scratch_shapes=[
            pltpu.VMEM((2, SWA_HEADS, SWA_BLOCK, SWA_BLOCK), F32),
            pltpu.VMEM((SWA_KV_HEADS, SWA_TQ + SWA_BLOCK, LANES), BF16),
            pltpu.VMEM((SWA_KV_HEADS, 2, SWA_TQ + SWA_BLOCK, 2 * LANES), BF16),
        ],
        compiler_params=_params("arbitrary", "arbitrary"),
        name="swa_core",
    )(rel_bias, sinks, jnp.asarray(_t5_bucket_table()), z, z, z)


def _swa_mixer(x, norm_g, w_qkv, b_qkv, w_o, b_o, sinks, rel_bias, batch):
    t = x.shape[0]
    z = _norm_proj(x, norm_g, w_qkv.astype(BF16), b_qkv, BF16, "swa_qkv")
    o = _swa_core(z.reshape(batch, t // batch, SWA_Q_W + 2 * SWA_KV_W), rel_bias, sinks)
    return _proj_res(o.reshape(t, SWA_Q_W), w_o.astype(BF16), b_o, x, "swa_out")


def kernel(x, p, rel_bias, final_norm, ffn1_norm, ffn1_w_gate, ffn1_w_up, ffn1_w_down, mix_norm, ffn2_norm,
           ffn2_w_gate, ffn2_w_up, ffn2_w_down, ple_norm, ple_w_proj, ple_w_gate, hyb_w_in, hyb_w_out, gla_w_fup,
           gla_b_f, gla_norm, lru_conv_w, lru_conv_b, lru_w_a, lru_b_a, lru_w_x, lru_b_x, lru_lambda, swa_w_qkv,
           swa_b_qkv, swa_w_o, swa_b_o, swa_sinks):
    batch, seq, _ = x.shape
    t = batch * seq
    x = x.reshape(t, D_MODEL)
    p = p.reshape(DEPTH, t, PLE_DIM)
    ffn1_w_gate, ffn1_w_up, ffn1_w_down, ffn2_w_gate, ffn2_w_up, ffn2_w_down = (
        w.astype(BF16) for w in (ffn1_w_gate, ffn1_w_up, ffn1_w_down, ffn2_w_gate, ffn2_w_up, ffn2_w_down))
    for i in range(DEPTH):
        x = _ffn(x, ffn1_norm, ffn1_w_gate, ffn1_w_up, ffn1_w_down, i)
        if i % 2 == 0:
            e = i // 2
            x = _hybrid_mixer(x, mix_norm[i], hyb_w_in[e], hyb_w_out[e], gla_w_fup[e], gla_b_f[e], gla_norm[e],
                              lru_conv_w[e], lru_conv_b[e], lru_w_a[e], lru_b_a[e], lru_w_x[e], lru_b_x[e],
                              lru_lambda[e], batch)
        else:
            o = i // 2
            x = _swa_mixer(x, mix_norm[i], swa_w_qkv[o], swa_b_qkv[o], swa_w_o[o], swa_b_o[o], swa_sinks[o],
                           rel_bias, batch)
        x = _ffn(x, ffn2_norm, ffn2_w_gate, ffn2_w_up, ffn2_w_down, i)
        x = _ple(x, ple_norm[i], ple_w_gate[i].astype(BF16), p[i], ple_w_proj[i].astype(BF16), final_norm,
                 final=(i == DEPTH - 1))
    return x.reshape(batch, seq, D_MODEL)
```

```python
import functools
import math

import jax
import jax.numpy as jnp
import numpy as np
from jax import lax
from jax.experimental import pallas as pl
from jax.experimental.pallas import tpu as pltpu

F32 = jnp.float32
BF16 = jnp.bfloat16

D_MODEL = 1024
DEPTH = 4
PLE_DIM = 256
D_FF = 2816
EPS = 1e-6

GLA_HEADS = 4
GLA_DK = 64
GLA_DV = 128
GLA_KW = GLA_HEADS * GLA_DK
GLA_VW = GLA_HEADS * GLA_DV
GLA_GATE_RANK = 16
GLA_GATE_TAU = 16.0
GLA_CHUNK = 64
LRU_WIDTH = 512
LRU_BLOCKS = 8
LRU_CONV_W = 4
LRU_C = 8.0

SWA_HEADS = 16
SWA_KV_HEADS = 4
SWA_HEAD_DIM = 64
SWA_GROUP = SWA_HEADS // SWA_KV_HEADS
SWA_BLOCK = 128
SWA_Q_W = SWA_HEADS * SWA_HEAD_DIM
SWA_KV_W = SWA_KV_HEADS * SWA_HEAD_DIM
REL_BUCKETS = 32
REL_MAX_DIST = 128
MASK_VALUE = -1e30

LANES = 128
SUBLANES = 8
VMEM_LIMIT_BYTES = 56 * 1024 * 1024

TOK_TM = 512
FFN_TF = 256
HYB_TS = 256
SWA_TQ = 512

HYB_QP = GLA_HEADS * LANES
Z_Q = 0
Z_K = Z_Q + HYB_QP
Z_V = Z_K + HYB_QP
Z_R = Z_V + GLA_VW
Z_X = Z_R + GLA_VW
Z_G = Z_X + LRU_WIDTH
Z_F = Z_G + LRU_WIDTH
HYB_ZW = Z_F + LANES


def _params(*semantics):
    return pltpu.CompilerParams(dimension_semantics=semantics, vmem_limit_bytes=VMEM_LIMIT_BYTES)


def _rms(x, g):
    return x * lax.rsqrt(jnp.mean(x * x, axis=-1, keepdims=True) + EPS) * g


def _sigmoid(x):
    return 1.0 / (1.0 + jnp.exp(-x))


def _dot(a, b):
    return jnp.dot(a, b, preferred_element_type=F32)


def _dot_nt(a, b):
    return lax.dot_general(a, b, (((1,), (1,)), ((), ())), preferred_element_type=F32)


def _dot_tn(a, b):
    return lax.dot_general(a, b, (((0,), (0,)), ((), ())), preferred_element_type=F32)


def _ffn_apply(x, g_ref, wg_ref, wu_ref, wd_ref):
    xn = _rms(x, g_ref[...]).astype(BF16)
    acc = x
    for c0 in range(0, D_FF, FFN_TF):
        a = _dot(xn, wg_ref[:, c0:c0 + FFN_TF])
        b = _dot(xn, wu_ref[:, c0:c0 + FFN_TF])
        h = (0.5 * a) * _sigmoid(a) * b
        acc = acc + _dot(h.astype(BF16), wd_ref[c0:c0 + FFN_TF, :])
    return acc


def _pre_body(x_ref, g1_ref, wg_ref, wu_ref, wd_ref, gm_ref, win_ref, bin_ref, xo_ref, z_ref):
    x = _ffn_apply(x_ref[...], g1_ref, wg_ref, wu_ref, wd_ref)
    xo_ref[...] = x
    z_ref[...] = (_dot(_rms(x, gm_ref[...]).astype(BF16), win_ref[...]) + bin_ref[...]).astype(z_ref.dtype)


def _post_body(mix_ref, x_ref, p_ref, wo_ref, bo_ref, g2_ref, wg_ref, wu_ref, wd_ref, gp_ref, wpg_ref, wpp_ref,
               fg_ref, o_ref, *, final):
    x = x_ref[...] + (_dot(mix_ref[...], wo_ref[...]) + bo_ref[...])
    x = _ffn_apply(x, g2_ref, wg_ref, wu_ref, wd_ref)
    gate = _sigmoid(_dot(_rms(x, gp_ref[...]).astype(BF16), wpg_ref[...]))
    x = x + gate * _dot(p_ref[...].astype(BF16), wpp_ref[...])
    if final:
        x = _rms(x, fg_ref[...])
    o_ref[...] = x


def _rows(width):
    return pl.BlockSpec((TOK_TM, width), lambda i: (i, 0))


def _resident(array, layer=None):
    if layer is None:
        return pl.BlockSpec(array.shape, lambda i: (0, 0), pipeline_mode=pl.Buffered(1))
    return pl.BlockSpec((None,) + array.shape[1:], lambda i: (layer, 0, 0), pipeline_mode=pl.Buffered(1))


def _pre_stage(x, layer, ffn, mix_g, w_in, b_in, z_dtype, name):
    t = x.shape[0]
    n = w_in.shape[1]
    return pl.pallas_call(
        _pre_body,
        grid=(t // TOK_TM,),
        in_specs=[_rows(D_MODEL)] + [_resident(w, layer) for w in ffn]
        + [_resident(mix_g, layer), _resident(w_in), _resident(b_in)],
        out_specs=[_rows(D_MODEL), _rows(n)],
        out_shape=[jax.ShapeDtypeStruct((t, D_MODEL), F32), jax.ShapeDtypeStruct((t, n), z_dtype)],
        compiler_params=_params("parallel"),
        name=name,
    )(x, *ffn, mix_g, w_in, b_in)


def _post_stage(mix, x, p, layer, w_o, b_o, ffn, ple, final_g, name):
    t = x.shape[0]
    return pl.pallas_call(
        functools.partial(_post_body, final=(layer == DEPTH - 1)),
        grid=(t // TOK_TM,),
        in_specs=[_rows(mix.shape[1]), _rows(D_MODEL),
                  pl.BlockSpec((None, TOK_TM, PLE_DIM), lambda i: (layer, i, 0)),
                  _resident(w_o), _resident(b_o)]
        + [_resident(w, layer) for w in ffn] + [_resident(w, layer) for w in ple] + [_resident(final_g)],
        out_specs=_rows(D_MODEL),
        out_shape=jax.ShapeDtypeStruct((t, D_MODEL), F32),
        compiler_params=_params("parallel"),
        name=name,
    )(mix, x, p, w_o, b_o, *ffn, *ple, final_g)


def _split3(x):
    hi = x.astype(BF16)
    r1 = x - hi.astype(F32)
    mid = r1.astype(BF16)
    lo = (r1 - mid.astype(F32)).astype(BF16)
    return hi, mid, lo


def _hybrid_body(z_ref, wf_ref, bf_ref, gn_ref, cw_ref, cb_ref, wa_ref, ba_ref, wx_ref, bx_ref, lam_ref,
                 o_ref, st_ref, xbuf_ref, a_ref, u_ref, h_ref, hc_ref):
    ts = HYB_TS
    c_len = GLA_CHUNK

    @pl.when(pl.program_id(1) == 0)
    def _():
        st_ref[...] = jnp.zeros_like(st_ref)
        xbuf_ref[0:SUBLANES, :] = jnp.zeros((SUBLANES, LRU_WIDTH), F32)
        hc_ref[...] = jnp.zeros_like(hc_ref)

    f_low = z_ref[:, Z_F:Z_F + LANES].astype(BF16)
    gate_in = _dot(f_low, wf_ref[...]) + bf_ref[...]
    log_f = (jnp.minimum(gate_in, 0.0) - jnp.log1p(jnp.exp(-jnp.abs(gate_in)))) * (1.0 / GLA_GATE_TAU)
    row = lax.broadcasted_iota(jnp.int32, (ts, ts), 0)
    col = lax.broadcasted_iota(jnp.int32, (ts, ts), 1)
    tri = jnp.where((row // c_len == col // c_len) & (col <= row), 1.0, 0.0).astype(BF16)
    hi, mid, lo = _split3(log_f)
    b_all = _dot(tri, hi) + _dot(tri, mid) + _dot(tri, lo)

    crow = lax.broadcasted_iota(jnp.int32, (c_len, c_len), 0)
    ccol = lax.broadcasted_iota(jnp.int32, (c_len, c_len), 1)
    causal = ccol <= crow
    for c in range(ts // c_len):
        r0 = c * c_len
        b_c = b_all[r0:r0 + c_len]
        b_last = b_c[c_len - 1:c_len]
        q_c = z_ref[r0:r0 + c_len, Z_Q:Z_Q + HYB_QP]
        k_c = z_ref[r0:r0 + c_len, Z_K:Z_K + HYB_QP]
        q_dec = (q_c * (GLA_DK ** -0.5) * jnp.exp(b_c)).astype(BF16)
        k_dec = (k_c * jnp.exp(-b_c)).astype(BF16)
        k_end = (k_c * jnp.exp(b_last - b_c)).astype(BF16)
        decay = jnp.exp(b_last)
        v_c = z_ref[r0:r0 + c_len, Z_V:Z_V + GLA_VW].astype(BF16)
        r_c = z_ref[r0:r0 + c_len, Z_R:Z_R + GLA_VW]
        for hd in range(GLA_HEADS):
            kl = slice(hd * LANES, (hd + 1) * LANES)
            vl = slice(hd * GLA_DV, (hd + 1) * GLA_DV)
            att = jnp.where(causal, _dot_nt(q_dec[:, kl], k_dec[:, kl]), 0.0)
            s_t = st_ref[hd]
            o = _dot(att.astype(BF16), v_c[:, vl]) + _dot_nt(q_dec[:, kl], s_t.astype(BF16))
            st_ref[hd] = s_t * decay[:, kl] + _dot_tn(v_c[:, vl], k_end[:, kl])
            o = o * lax.rsqrt(jnp.mean(o * o, axis=-1, keepdims=True) + EPS)
            r_h = r_c[:, vl]
            o = o * gn_ref[:, vl] * (r_h * _sigmoid(r_h))
            o_ref[r0:r0 + c_len, vl] = o.astype(o_ref.dtype)

    xbuf_ref[SUBLANES:SUBLANES + ts, :] = z_ref[:, Z_X:Z_X + LRU_WIDTH]
    xc = xbuf_ref[SUBLANES - 3:SUBLANES - 3 + ts, :] * cw_ref[0:1, :]
    for tap in range(1, LRU_CONV_W):
        off = SUBLANES - (LRU_CONV_W - 1) + tap
        xc = xc + xbuf_ref[off:off + ts, :] * cw_ref[tap:tap + 1, :]
    xc = xc + cb_ref[...]
    xbuf_ref[0:SUBLANES, :] = xbuf_ref[ts:ts + SUBLANES, :]

    xcb = xc.astype(BF16)
    r_gate = _sigmoid(_dot(xcb, wa_ref[...]) + ba_ref[...])
    i_gate = _sigmoid(_dot(xcb, wx_ref[...]) + bx_ref[...])
    neg_lam = -lam_ref[...]
    softplus = jnp.maximum(neg_lam, 0.0) + jnp.log1p(jnp.exp(-jnp.abs(neg_lam)))
    log_a = -LRU_C * r_gate * softplus
    a_ref[...] = jnp.exp(log_a)
    th = jnp.tanh(log_a)
    u_ref[...] = jnp.sqrt(-2.0 * th / (1.0 - th)) * (i_gate * xc)

    srow = lax.broadcasted_iota(jnp.int32, (SUBLANES, LRU_WIDTH), 0)

    def scan_group(g, carry):
        r0 = pl.multiple_of(g * SUBLANES, SUBLANES)
        a = a_ref[pl.ds(r0, SUBLANES), :]
        u = u_ref[pl.ds(r0, SUBLANES), :]
        for s in (1, 2, 4):
            keep = srow >= s
            a_sh = jnp.where(keep, pltpu.roll(a, s, 0), 1.0)
            u_sh = jnp.where(keep, pltpu.roll(u, s, 0), 0.0)
            u = a * u_sh + u
            a = a * a_sh
        h = a * carry + u
        h_ref[pl.ds(r0, SUBLANES), :] = h
        return h[SUBLANES - 1:SUBLANES, :]

    hc_ref[...] = lax.fori_loop(0, ts // SUBLANES, scan_group, hc_ref[...], unroll=4)

    g_in = z_ref[:, Z_G:Z_G + LRU_WIDTH]
    gelu = 0.5 * g_in * (1.0 + jnp.tanh(math.sqrt(2.0 / math.pi) * (g_in + 0.044715 * (g_in * g_in * g_in))))
    o_ref[:, GLA_VW:GLA_VW + LRU_WIDTH] = (h_ref[...] * gelu).astype(o_ref.dtype)


def _hybrid_core(z, wf, bf, gn, cw, cb, wa, ba, wx, bx, lam):
    b, s, _ = z.shape
    const = lambda shape: pl.BlockSpec(shape, lambda i, j: (0,) * len(shape))
    return pl.pallas_call(
        _hybrid_body,
        grid=(b, s // HYB_TS),
        in_specs=[
            pl.BlockSpec((None, HYB_TS, HYB_ZW), lambda i, j: (i, j, 0)),
            const((LANES, HYB_QP)), const((1, HYB_QP)), const((1, GLA_VW)),
            const((LRU_CONV_W, LRU_WIDTH)), const((1, LRU_WIDTH)),
            const((LRU_WIDTH, LRU_WIDTH)), const((1, LRU_WIDTH)),
            const((LRU_WIDTH, LRU_WIDTH)), const((1, LRU_WIDTH)),
            const((1, LRU_WIDTH)),
        ],
        out_specs=pl.BlockSpec((None, HYB_TS, GLA_VW + LRU_WIDTH), lambda i, j: (i, j, 0)),
        out_shape=jax.ShapeDtypeStruct((b, s, GLA_VW + LRU_WIDTH), BF16),
        scratch_shapes=[
            pltpu.VMEM((GLA_HEADS, GLA_DV, LANES), F32),
            pltpu.VMEM((HYB_TS + 2 * SUBLANES, LRU_WIDTH), F32),
            pltpu.VMEM((HYB_TS, LRU_WIDTH), F32),
            pltpu.VMEM((HYB_TS, LRU_WIDTH), F32),
            pltpu.VMEM((HYB_TS, LRU_WIDTH), F32),
            pltpu.VMEM((1, LRU_WIDTH), F32),
        ],
        compiler_params=_params("parallel", "arbitrary"),
        name="hybrid_core",
    )(z, wf, bf, gn, cw, cb, wa, ba, wx, bx, lam)


def _pad_heads(w):
    rows = w.shape[0]
    w = w.reshape(rows, GLA_HEADS, GLA_DK)
    return jnp.pad(w, ((0, 0), (0, 0), (0, LANES - GLA_DK))).reshape(rows, GLA_HEADS * LANES)


def _block_diag(w):
    g, bw, _ = w.shape
    eye = jnp.eye(g, dtype=w.dtype)
    return (eye[:, None, :, None] * w[:, :, None, :]).reshape(g * bw, g * bw)


def _hybrid_in_weights(w_in):
    q_w, k_w, v_w, r_w, f_w, xr_w, gr_w = jnp.split(
        w_in, np.cumsum([GLA_KW, GLA_KW, GLA_VW, GLA_VW, GLA_GATE_RANK, LRU_WIDTH]).tolist(), axis=1)
    f_w = jnp.pad(f_w, ((0, 0), (0, LANES - GLA_GATE_RANK)))
    return jnp.concatenate([_pad_heads(q_w), _pad_heads(k_w), v_w, r_w, xr_w, gr_w, f_w], axis=1).astype(BF16)


def _hybrid_mix(z, w_fup, b_f, gla_norm, conv_w, conv_b, w_a, b_a, w_x, b_x, lam, batch):
    t = z.shape[0]
    wf = jnp.pad(_pad_heads(w_fup), ((0, LANES - GLA_GATE_RANK), (0, 0))).astype(BF16)
    mix = _hybrid_core(
        z.reshape(batch, t // batch, HYB_ZW), wf, _pad_heads(b_f.reshape(1, GLA_KW)), gla_norm.reshape(1, GLA_VW),
        conv_w, conv_b.reshape(1, LRU_WIDTH), _block_diag(w_a).astype(BF16), b_a.reshape(1, LRU_WIDTH),
        _block_diag(w_x).astype(BF16), b_x.reshape(1, LRU_WIDTH), lam.reshape(1, LRU_WIDTH))
    return mix.reshape(t, GLA_VW + LRU_WIDTH)


def _t5_bucket_table():
    max_exact = REL_BUCKETS // 2
    dist = np.arange(SWA_BLOCK)
    d = np.maximum(dist, 1).astype(np.float32)
    large = max_exact + (np.log(d / max_exact) / math.log(REL_MAX_DIST / max_exact)
                         * (REL_BUCKETS - max_exact)).astype(np.int32)
    bucket = np.where(dist < max_exact, dist, np.minimum(large, REL_BUCKETS - 1)).astype(np.int32)
    i = np.arange(SWA_BLOCK)[:, None]
    j = np.arange(SWA_BLOCK)[None, :]
    return bucket[(i - j) % SWA_BLOCK]


def _roll_half_lanes(x):
    words = pltpu.bitcast(x, jnp.int32)
    return pltpu.bitcast(pltpu.roll(words, LANES // 2, 1), BF16)


def _swa_body(rel_ref, sink_ref, bkt_ref, q_ref, kvc_ref, kvp_ref, o_ref, bias_ref, kd_ref, vo_ref):
    blk = SWA_BLOCK
    dh = SWA_HEAD_DIM
    tq = SWA_TQ
    first_tile = pl.program_id(1) == 0
    row = lax.broadcasted_iota(jnp.int32, (blk, blk), 0)
    col = lax.broadcasted_iota(jnp.int32, (blk, blk), 1)
    in_cur = col <= row
    lo_half = col < dh

    @pl.when((pl.program_id(0) == 0) & first_tile)
    def _():
        bkt = bkt_ref[...]
        for h in range(SWA_HEADS):
            acc = jnp.zeros((blk, blk), F32)
            for k in range(REL_BUCKETS):
                acc = jnp.where(bkt == k, rel_ref[k, h], acc)
            bias_ref[0, h] = acc
            bias_ref[1, h] = jnp.where(in_cur, acc, MASK_VALUE)
        band_lane = lax.broadcasted_iota(jnp.int32, (tq + blk, LANES), 1)
        ones_lo = jnp.where(band_lane < dh, 1.0, 0.0).astype(BF16)
        for kv in range(SWA_KV_HEADS):
            vo_ref[kv, 0, :, LANES:] = ones_lo
            vo_ref[kv, 1, :, LANES:] = 1.0 - ones_lo

    def build_band(src_ref, r0, rows):
        lo = lax.broadcasted_iota(jnp.int32, (rows, LANES), 1) < dh
        zero = jnp.zeros((rows, LANES), BF16)
        for kv in range(SWA_KV_HEADS):
            c0 = LANES * (kv // 2)
            k_pair = src_ref[:, c0:c0 + LANES]
            v_pair = src_ref[:, SWA_KV_W + c0:SWA_KV_W + c0 + LANES]
            k_rot = _roll_half_lanes(k_pair)
            v_rot = _roll_half_lanes(v_pair)
            k_own, k_other, v_own, v_other = ((k_pair, k_rot, v_pair, v_rot) if kv % 2 == 0
                                              else (k_rot, k_pair, v_rot, v_pair))
            kd_ref[kv, r0:r0 + rows, :] = jnp.where(lo, k_own, k_other)
            vo_ref[kv, 0, r0:r0 + rows, :LANES] = jnp.where(lo, v_own, zero)
            vo_ref[kv, 1, r0:r0 + rows, :LANES] = jnp.where(lo, zero, v_other)

    build_band(kvp_ref, 0, blk)
    build_band(kvc_ref, blk, tq)

    lane = lax.broadcasted_iota(jnp.int32, (1, LANES), 1)
    q_scale = (jnp.where(lane < dh, dh ** -0.5, 0.0).astype(BF16), jnp.where(lane < dh, 0.0, dh ** -0.5).astype(BF16))
    zero_p = jnp.zeros((blk, blk), BF16)

    def sub_block(n, carry):
        r0 = pl.multiple_of(n * blk, blk)
        first = jnp.where(first_tile & (n == 0), 1, 0)
        scores = []
        for kv in range(SWA_KV_HEADS):
            k_band = kd_ref[kv, pl.ds(r0, 2 * blk), :]
            for pr in range(SWA_GROUP // 2):
                q_pair = q_ref[pl.ds(r0, blk), pl.ds(LANES * (2 * kv + pr), LANES)]
                for half in range(2):
                    h = SWA_GROUP * kv + 2 * pr + half
                    s = _dot_nt(q_pair * q_scale[half], k_band)
                    scores.append(jnp.where(in_cur, s[:, blk:], s[:, :blk]) + bias_ref[first, h])
        maxes = [jnp.maximum(jnp.max(s, axis=-1, keepdims=True), sink_ref[h]) for h, s in enumerate(scores)]
        exps = [jnp.exp(s - m).astype(BF16) for s, m in zip(scores, maxes)]
        for kv in range(SWA_KV_HEADS):
            for pr in range(SWA_GROUP // 2):
                h0 = SWA_GROUP * kv + 2 * pr
                acc = None
                for half in range(2):
                    e = exps[h0 + half]
                    p_band = jnp.concatenate([jnp.where(in_cur, zero_p, e), jnp.where(in_cur, e, zero_p)], axis=1)
                    part = _dot(p_band, vo_ref[kv, half, pl.ds(r0, 2 * blk), :])
                    acc = part if acc is None else acc + part
                sink_term = jnp.where(lo_half, jnp.exp(sink_ref[h0] - maxes[h0]),
                                      jnp.exp(sink_ref[h0 + 1] - maxes[h0 + 1]))
                out = acc[:, :LANES] / (acc[:, LANES:] + sink_term)
                o_ref[pl.ds(r0, blk), pl.ds(LANES * (2 * kv + pr), LANES)] = out.astype(o_ref.dtype)
        return carry

    lax.fori_loop(0, tq // blk, sub_block, 0)


def _swa_core(z, rel_bias, sinks):
    b, s, _ = z.shape
    kv_col = SWA_Q_W // (2 * SWA_KV_W)
    blocks_per_tile = SWA_TQ // SWA_BLOCK
    smem = lambda: pl.BlockSpec(memory_space=pltpu.SMEM)
    return pl.pallas_call(
        _swa_body,
        grid=(b, s // SWA_TQ),
        in_specs=[
            smem(), smem(),
            pl.BlockSpec((SWA_BLOCK, SWA_BLOCK), lambda i, j: (0, 0)),
            pl.BlockSpec((None, SWA_TQ, SWA_Q_W), lambda i, j: (i, j, 0)),
            pl.BlockSpec((None, SWA_TQ, 2 * SWA_KV_W), lambda i, j: (i, j, kv_col)),
            pl.BlockSpec((None, SWA_BLOCK, 2 * SWA_KV_W),
                         lambda i, j: (i, jnp.maximum(j * blocks_per_tile - 1, 0), kv_col)),
        ],
        out_specs=pl.BlockSpec((None, SWA_TQ, SWA_Q_W), lambda i, j: (i, j, 0)),
        out_shape=jax.ShapeDtypeStruct((b, s, SWA_Q_W), BF16),
        scratch_shapes=[
            pltpu.VMEM((2, SWA_HEADS, SWA_BLOCK, SWA_BLOCK), F32),
            pltpu.VMEM((SWA_KV_HEADS, SWA_TQ + SWA_BLOCK, LANES), BF16),
            pltpu.VMEM((SWA_KV_HEADS, 2, SWA_TQ + SWA_BLOCK, 2 * LANES), BF16),
        ],
        compiler_params=_params("arbitrary", "arbitrary"),
        name="swa_core",
    )(rel_bias, sinks, jnp.asarray(_t5_bucket_table()), z, z, z)


def _swa_mix(z, rel_bias, sinks, batch):
    t = z.shape[0]
    return _swa_core(z.reshape(batch, t // batch, SWA_Q_W + 2 * SWA_KV_W), rel_bias, sinks).reshape(t, SWA_Q_W)


def kernel(x, p, rel_bias, final_norm, ffn1_norm, ffn1_w_gate, ffn1_w_up, ffn1_w_down, mix_norm, ffn2_norm,
           ffn2_w_gate, ffn2_w_up, ffn2_w_down, ple_norm, ple_w_proj, ple_w_gate, hyb_w_in, hyb_w_out, gla_w_fup,
           gla_b_f, gla_norm, lru_conv_w, lru_conv_b, lru_w_a, lru_b_a, lru_w_x, lru_b_x, lru_lambda, swa_w_qkv,
           swa_b_qkv, swa_w_o, swa_b_o, swa_sinks):
    batch, seq, _ = x.shape
    t = batch * seq
    x = x.reshape(t, D_MODEL)
    p = p.reshape(DEPTH, t, PLE_DIM)
    gains = lambda g: g.reshape(DEPTH, 1, D_MODEL)
    ffn1 = (gains(ffn1_norm), ffn1_w_gate.astype(BF16), ffn1_w_up.astype(BF16), ffn1_w_down.astype(BF16))
    ffn2 = (gains(ffn2_norm), ffn2_w_gate.astype(BF16), ffn2_w_up.astype(BF16), ffn2_w_down.astype(BF16))
    ple = (gains(ple_norm), ple_w_gate.astype(BF16), ple_w_proj.astype(BF16))
    mix_g = gains(mix_norm)
    final_g = final_norm.reshape(1, D_MODEL)
    no_bias_in = jnp.zeros((1, HYB_ZW), F32)
    no_bias_out = jnp.zeros((1, D_MODEL), F32)
    for i in range(DEPTH):
        if i % 2 == 0:
            e = i // 2
            x, z = _pre_stage(x, i, ffn1, mix_g, _hybrid_in_weights(hyb_w_in[e]), no_bias_in, F32, "pre_hyb")
            mix = _hybrid_mix(z, gla_w_fup[e], gla_b_f[e], gla_norm[e], lru_conv_w[e], lru_conv_b[e], lru_w_a[e],
                              lru_b_a[e], lru_w_x[e], lru_b_x[e], lru_lambda[e], batch)
            x = _post_stage(mix, x, p, i, hyb_w_out[e].astype(BF16), no_bias_out, ffn2, ple, final_g, "post_hyb")
        else:
            o = i // 2
            x, z = _pre_stage(x, i, ffn1, mix_g, swa_w_qkv[o].astype(BF16), swa_b_qkv[o].reshape(1, -1), BF16,
                              "pre_swa")
            mix = _swa_mix(z, rel_bias, swa_sinks[o], batch)
            x = _post_stage(mix, x, p, i, swa_w_o[o].astype(BF16), swa_b_o[o].reshape(1, D_MODEL), ffn2, ple,
                            final_g, "post_swa")
    return x.reshape(batch, seq, D_MODEL)
```

```python
import functools
import math

import jax
import jax.numpy as jnp
import numpy as np
from jax import lax
from jax.experimental import pallas as pl
from jax.experimental.pallas import tpu as pltpu

F32 = jnp.float32
BF16 = jnp.bfloat16

D_MODEL = 1024
DEPTH = 4
PLE_DIM = 256
D_FF = 2816
EPS = 1e-6

GLA_HEADS = 4
GLA_DK = 64
GLA_DV = 128
GLA_KW = GLA_HEADS * GLA_DK
GLA_VW = GLA_HEADS * GLA_DV
GLA_GATE_RANK = 16
GLA_GATE_TAU = 16.0
GLA_CHUNK = 64
LRU_WIDTH = 512
LRU_BLOCKS = 8
LRU_CONV_W = 4
LRU_C = 8.0

SWA_HEADS = 16
SWA_KV_HEADS = 4
SWA_HEAD_DIM = 64
SWA_GROUP = SWA_HEADS // SWA_KV_HEADS
SWA_BLOCK = 128
SWA_Q_W = SWA_HEADS * SWA_HEAD_DIM
SWA_KV_W = SWA_KV_HEADS * SWA_HEAD_DIM
REL_BUCKETS = 32
REL_MAX_DIST = 128
MASK_VALUE = -1e30

LANES = 128
SUBLANES = 8
VMEM_LIMIT_BYTES = 56 * 1024 * 1024

TOK_TM = 512
FFN_TF = 256
HYB_TS = 256
SWA_TQ = 512

Z_Q = 0
Z_K = Z_Q + GLA_KW
Z_V = Z_K + GLA_KW
Z_R = Z_V + GLA_VW
Z_X = Z_R + GLA_VW
Z_G = Z_X + LRU_WIDTH
Z_F = Z_G + LRU_WIDTH
HYB_ZW = Z_F + LANES


def _params(*semantics):
    return pltpu.CompilerParams(dimension_semantics=semantics, vmem_limit_bytes=VMEM_LIMIT_BYTES)


def _rms(x, g):
    return x * lax.rsqrt(jnp.mean(x * x, axis=-1, keepdims=True) + EPS) * g


def _sigmoid(x):
    return 1.0 / (1.0 + jnp.exp(-x))


def _dot(a, b):
    return jnp.dot(a, b, preferred_element_type=F32)


def _dot_nt(a, b):
    return lax.dot_general(a, b, (((1,), (1,)), ((), ())), preferred_element_type=F32)


def _dot_tn(a, b):
    return lax.dot_general(a, b, (((0,), (0,)), ((), ())), preferred_element_type=F32)


def _ffn_apply(x, g_ref, wg_ref, wu_ref, wd_ref):
    xn = _rms(x, g_ref[...]).astype(BF16)
    acc = x
    for c0 in range(0, D_FF, FFN_TF):
        a = _dot(xn, wg_ref[:, c0:c0 + FFN_TF])
        b = _dot(xn, wu_ref[:, c0:c0 + FFN_TF])
        h = (0.5 * a) * _sigmoid(a) * b
        acc = acc + _dot(h.astype(BF16), wd_ref[c0:c0 + FFN_TF, :])
    return acc


def _pre_body(x_ref, g1_ref, wg_ref, wu_ref, wd_ref, gm_ref, win_ref, bin_ref, xo_ref, z_ref):
    x = _ffn_apply(x_ref[...], g1_ref, wg_ref, wu_ref, wd_ref)
    xo_ref[...] = x
    z_ref[...] = (_dot(_rms(x, gm_ref[...]).astype(BF16), win_ref[...]) + bin_ref[...]).astype(z_ref.dtype)


def _post_body(mix_ref, x_ref, p_ref, wo_ref, bo_ref, g2_ref, wg_ref, wu_ref, wd_ref, gp_ref, wpg_ref, wpp_ref,
               fg_ref, o_ref, *, final):
    x = x_ref[...] + (_dot(mix_ref[...], wo_ref[...]) + bo_ref[...])
    x = _ffn_apply(x, g2_ref, wg_ref, wu_ref, wd_ref)
    gate = _sigmoid(_dot(_rms(x, gp_ref[...]).astype(BF16), wpg_ref[...]))
    x = x + gate * _dot(p_ref[...].astype(BF16), wpp_ref[...])
    if final:
        x = _rms(x, fg_ref[...])
    o_ref[...] = x


def _rows(width):
    return pl.BlockSpec((TOK_TM, width), lambda i: (i, 0))


def _resident(array, layer=None):
    if layer is None:
        return pl.BlockSpec(array.shape, lambda i: (0, 0), pipeline_mode=pl.Buffered(1))
    return pl.BlockSpec((None,) + array.shape[1:], lambda i: (layer, 0, 0), pipeline_mode=pl.Buffered(1))


def _pre_stage(x, layer, ffn, mix_g, w_in, b_in, z_dtype, name):
    t = x.shape[0]
    n = w_in.shape[1]
    return pl.pallas_call(
        _pre_body,
        grid=(t // TOK_TM,),
        in_specs=[_rows(D_MODEL)] + [_resident(w, layer) for w in ffn]
        + [_resident(mix_g, layer), _resident(w_in), _resident(b_in)],
        out_specs=[_rows(D_MODEL), _rows(n)],
        out_shape=[jax.ShapeDtypeStruct((t, D_MODEL), F32), jax.ShapeDtypeStruct((t, n), z_dtype)],
        compiler_params=_params("parallel"),
        name=name,
    )(x, *ffn, mix_g, w_in, b_in)


def _post_stage(mix, x, p, layer, w_o, b_o, ffn, ple, final_g, name):
    t = x.shape[0]
    return pl.pallas_call(
        functools.partial(_post_body, final=(layer == DEPTH - 1)),
        grid=(t // TOK_TM,),
        in_specs=[_rows(mix.shape[1]), _rows(D_MODEL),
                  pl.BlockSpec((None, TOK_TM, PLE_DIM), lambda i: (layer, i, 0)),
                  _resident(w_o), _resident(b_o)]
        + [_resident(w, layer) for w in ffn] + [_resident(w, layer) for w in ple] + [_resident(final_g)],
        out_specs=_rows(D_MODEL),
        out_shape=jax.ShapeDtypeStruct((t, D_MODEL), F32),
        compiler_params=_params("parallel"),
        name=name,
    )(mix, x, p, w_o, b_o, *ffn, *ple, final_g)


def _split3(x):
    hi = x.astype(BF16)
    r1 = x - hi.astype(F32)
    mid = r1.astype(BF16)
    lo = (r1 - mid.astype(F32)).astype(BF16)
    return hi, mid, lo


def _hybrid_body(z_ref, wf_ref, bf_ref, gn_ref, cw_ref, cb_ref, wa_ref, ba_ref, wx_ref, bx_ref, lam_ref,
                 o_ref, st_ref, xbuf_ref, hc_ref):
    ts = HYB_TS
    c_len = GLA_CHUNK
    n_chunks = ts // c_len
    half_w = LRU_WIDTH // 2

    @pl.when(pl.program_id(1) == 0)
    def _():
        st_ref[...] = jnp.zeros_like(st_ref)
        xbuf_ref[0:SUBLANES, :] = jnp.zeros((SUBLANES, LRU_WIDTH), F32)
        hc_ref[...] = jnp.zeros_like(hc_ref)

    xbuf_ref[SUBLANES:SUBLANES + ts, :] = z_ref[:, Z_X:Z_X + LRU_WIDTH]
    xc = xbuf_ref[SUBLANES - 3:SUBLANES - 3 + ts, :] * cw_ref[0:1, :]
    for tap in range(1, LRU_CONV_W):
        off = SUBLANES - (LRU_CONV_W - 1) + tap
        xc = xc + xbuf_ref[off:off + ts, :] * cw_ref[tap:tap + 1, :]
    xc = xc + cb_ref[...]
    xbuf_ref[0:SUBLANES, :] = xbuf_ref[ts:ts + SUBLANES, :]
    xcb = xc.astype(BF16)
    r_pre = [_dot(xcb[:, d * half_w:(d + 1) * half_w], wa_ref[d]) for d in range(2)]
    i_pre = [_dot(xcb[:, d * half_w:(d + 1) * half_w], wx_ref[d]) for d in range(2)]
    neg_lam = -lam_ref[...]
    softplus = jnp.maximum(neg_lam, 0.0) + jnp.log1p(jnp.exp(-jnp.abs(neg_lam)))

    def lru_inputs(r0, rows):
        rs = slice(r0, r0 + rows)
        r_gate = _sigmoid(jnp.concatenate([r_pre[0][rs], r_pre[1][rs]], axis=1) + ba_ref[...])
        i_gate = _sigmoid(jnp.concatenate([i_pre[0][rs], i_pre[1][rs]], axis=1) + bx_ref[...])
        log_a = -LRU_C * r_gate * softplus
        th = jnp.tanh(log_a)
        return jnp.exp(log_a), jnp.sqrt(-2.0 * th / (1.0 - th)) * (i_gate * xc[rs])

    f_low = z_ref[:, Z_F:Z_F + LANES].astype(BF16)
    gate_in = _dot(f_low, wf_ref[...]) + bf_ref[...]
    log_f = (jnp.minimum(gate_in, 0.0) - jnp.log1p(jnp.exp(-jnp.abs(gate_in)))) * (1.0 / GLA_GATE_TAU)
    row = lax.broadcasted_iota(jnp.int32, (ts, ts), 0)
    col = lax.broadcasted_iota(jnp.int32, (ts, ts), 1)
    in_chunk_causal = (row // c_len == col // c_len) & (col <= row)
    tri = jnp.where(in_chunk_causal, 1.0, 0.0).astype(BF16)
    hi, mid, lo = _split3(log_f)
    b_all = _dot(tri, hi) + _dot(tri, mid) + _dot(tri, lo)
    b_last = [b_all[(c + 1) * c_len - 1:(c + 1) * c_len] for c in range(n_chunks)]
    b_last_rows = jnp.concatenate([jnp.broadcast_to(bl, (c_len, GLA_KW)) for bl in b_last], axis=0)
    decay = [jnp.exp(bl) for bl in b_last]
    q_all = z_ref[:, Z_Q:Z_Q + GLA_KW]
    k_all = z_ref[:, Z_K:Z_K + GLA_KW]
    q_dec = (q_all * (GLA_DK ** -0.5) * jnp.exp(b_all)).astype(BF16)
    k_dec = (k_all * jnp.exp(-b_all)).astype(BF16)
    k_end = (k_all * jnp.exp(b_last_rows - b_all)).astype(BF16)
    lane = lax.broadcasted_iota(jnp.int32, (1, LANES), 1)
    head_lanes = (jnp.where(lane < GLA_DK, 1.0, 0.0).astype(BF16), jnp.where(lane < GLA_DK, 0.0, 1.0).astype(BF16))
    row_chunk = lax.broadcasted_iota(jnp.int32, (ts, LANES), 0) // c_len
    zero_k = jnp.zeros((ts, LANES), BF16)

    a_parts, u_parts = [], []
    for pair in range(GLA_HEADS // 2):
        kl = slice(pair * LANES, (pair + 1) * LANES)
        q_pair = q_dec[:, kl]
        q_heads = jnp.concatenate([q_pair * head_lanes[0], q_pair * head_lanes[1]], axis=0)
        scores = _dot_nt(q_heads, k_dec[:, kl])
        k_end_pair = k_end[:, kl]
        k_end_blocks = jnp.concatenate(
            [jnp.where(row_chunk == c, k_end_pair, zero_k) for c in range(n_chunks)], axis=1)
        for half in range(2):
            hd = 2 * pair + half
            vl = slice(hd * GLA_DV, (hd + 1) * GLA_DV)
            q_h = q_heads[half * ts:(half + 1) * ts]
            att = jnp.where(in_chunk_causal, scores[half * ts:(half + 1) * ts], 0.0).astype(BF16)
            v_h = z_ref[:, Z_V + hd * GLA_DV:Z_V + (hd + 1) * GLA_DV].astype(BF16)
            o = _dot(att, v_h)
            kv_t = _dot_tn(v_h, k_end_blocks)
            s_t = st_ref[hd]
            o_inter = []
            for c in range(n_chunks):
                o_inter.append(_dot_nt(q_h[c * c_len:(c + 1) * c_len], s_t.astype(BF16)))
                s_t = s_t * decay[c][:, kl] + kv_t[:, c * LANES:(c + 1) * LANES]
            st_ref[hd] = s_t
            o = o + jnp.concatenate(o_inter, axis=0)
            o = o * lax.rsqrt(jnp.mean(o * o, axis=-1, keepdims=True) + EPS)
            r_h = z_ref[:, Z_R + hd * GLA_DV:Z_R + (hd + 1) * GLA_DV]
            o_ref[:, vl] = (o * gn_ref[:, vl] * (r_h * _sigmoid(r_h))).astype(o_ref.dtype)
            a_blk, u_blk = lru_inputs(hd * (ts // GLA_HEADS), ts // GLA_HEADS)
            a_parts.append(a_blk)
            u_parts.append(u_blk)

    a_all = jnp.concatenate(a_parts, axis=0)
    u_all = jnp.concatenate(u_parts, axis=0)
    srow = lax.broadcasted_iota(jnp.int32, (SUBLANES, LRU_WIDTH), 0)
    local = []
    for g in range(ts // SUBLANES):
        a = a_all[g * SUBLANES:(g + 1) * SUBLANES]
        u = u_all[g * SUBLANES:(g + 1) * SUBLANES]
        for s in (1, 2, 4):
            keep = srow >= s
            a_sh = jnp.where(keep, pltpu.roll(a, s, 0), 1.0)
            u_sh = jnp.where(keep, pltpu.roll(u, s, 0), 0.0)
            u = a * u_sh + u
            a = a * a_sh
        local.append((a, u))
    carry = hc_ref[...]
    h_groups = []
    for a, u in local:
        h_groups.append(a * carry + u)
        carry = a[SUBLANES - 1:SUBLANES, :] * carry + u[SUBLANES - 1:SUBLANES, :]
    hc_ref[...] = carry
    g_in = z_ref[:, Z_G:Z_G + LRU_WIDTH]
    gelu = 0.5 * g_in * (1.0 + jnp.tanh(math.sqrt(2.0 / math.pi) * (g_in + 0.044715 * (g_in * g_in * g_in))))
    o_ref[:, GLA_VW:GLA_VW + LRU_WIDTH] = (jnp.concatenate(h_groups, axis=0) * gelu).astype(o_ref.dtype)


def _hybrid_core(z, wf, bf, gn, cw, cb, wa, ba, wx, bx, lam):
    b, s, _ = z.shape
    const = lambda shape: pl.BlockSpec(shape, lambda i, j: (0,) * len(shape))
    return pl.pallas_call(
        _hybrid_body,
        grid=(b, s // HYB_TS),
        in_specs=[
            pl.BlockSpec((None, HYB_TS, HYB_ZW), lambda i, j: (i, j, 0)),
            const((LANES, GLA_KW)), const((1, GLA_KW)), const((1, GLA_VW)),
            const((LRU_CONV_W, LRU_WIDTH)), const((1, LRU_WIDTH)),
            const((2, LRU_WIDTH // 2, LRU_WIDTH // 2)), const((1, LRU_WIDTH)),
            const((2, LRU_WIDTH // 2, LRU_WIDTH // 2)), const((1, LRU_WIDTH)),
            const((1, LRU_WIDTH)),
        ],
        out_specs=pl.BlockSpec((None, HYB_TS, GLA_VW + LRU_WIDTH), lambda i, j: (i, j, 0)),
        out_shape=jax.ShapeDtypeStruct((b, s, GLA_VW + LRU_WIDTH), BF16),
        scratch_shapes=[
            pltpu.VMEM((GLA_HEADS, GLA_DV, LANES), F32),
            pltpu.VMEM((HYB_TS + 2 * SUBLANES, LRU_WIDTH), F32),
            pltpu.VMEM((1, LRU_WIDTH), F32),
        ],
        compiler_params=_params("parallel", "arbitrary"),
        name="hybrid_core",
    )(z, wf, bf, gn, cw, cb, wa, ba, wx, bx, lam)


def _block_diag_halves(w):
    g, bw, _ = w.shape
    eye = jnp.eye(g // 2, dtype=w.dtype)
    w = w.reshape(2, g // 2, bw, bw)
    return (eye[None, :, None, :, None] * w[:, :, :, None, :]).reshape(2, g * bw // 2, g * bw // 2)


def _hybrid_in_weights(w_in):
    q_w, k_w, v_w, r_w, f_w, xr_w, gr_w = jnp.split(
        w_in, np.cumsum([GLA_KW, GLA_KW, GLA_VW, GLA_VW, GLA_GATE_RANK, LRU_WIDTH]).tolist(), axis=1)
    f_w = jnp.pad(f_w, ((0, 0), (0, LANES - GLA_GATE_RANK)))
    return jnp.concatenate([q_w, k_w, v_w, r_w, xr_w, gr_w, f_w], axis=1).astype(BF16)


def _hybrid_mix(z, w_fup, b_f, gla_norm, conv_w, conv_b, w_a, b_a, w_x, b_x, lam, batch):
    t = z.shape[0]
    wf = jnp.pad(w_fup, ((0, LANES - GLA_GATE_RANK), (0, 0))).astype(BF16)
    mix = _hybrid_core(
        z.reshape(batch, t // batch, HYB_ZW), wf, b_f.reshape(1, GLA_KW), gla_norm.reshape(1, GLA_VW),
        conv_w, conv_b.reshape(1, LRU_WIDTH), _block_diag_halves(w_a).astype(BF16), b_a.reshape(1, LRU_WIDTH),
        _block_diag_halves(w_x).astype(BF16), b_x.reshape(1, LRU_WIDTH), lam.reshape(1, LRU_WIDTH))
    return mix.reshape(t, GLA_VW + LRU_WIDTH)


def _t5_bucket_table():
    max_exact = REL_BUCKETS // 2
    dist = np.arange(SWA_BLOCK)
    d = np.maximum(dist, 1).astype(np.float32)
    large = max_exact + (np.log(d / max_exact) / math.log(REL_MAX_DIST / max_exact)
                         * (REL_BUCKETS - max_exact)).astype(np.int32)
    bucket = np.where(dist < max_exact, dist, np.minimum(large, REL_BUCKETS - 1)).astype(np.int32)
    i = np.arange(SWA_BLOCK)[:, None]
    j = np.arange(SWA_BLOCK)[None, :]
    return bucket[(i - j) % SWA_BLOCK]


def _roll_half_lanes(x):
    words = pltpu.bitcast(x, jnp.int32)
    return pltpu.bitcast(pltpu.roll(words, LANES // 2, 1), BF16)


def _swa_body(rel_ref, sink_ref, bkt_ref, q_ref, kvc_ref, kvp_ref, o_ref, bias_ref, kd_ref, vo_ref):
    blk = SWA_BLOCK
    dh = SWA_HEAD_DIM
    tq = SWA_TQ
    first_tile = pl.program_id(1) == 0
    row = lax.broadcasted_iota(jnp.int32, (blk, blk), 0)
    col = lax.broadcasted_iota(jnp.int32, (blk, blk), 1)
    in_cur = col <= row
    lo_half = col < dh

    @pl.when((pl.program_id(0) == 0) & first_tile)
    def _():
        bkt = bkt_ref[...]
        for h in range(SWA_HEADS):
            acc = jnp.zeros((blk, blk), F32)
            for k in range(REL_BUCKETS):
                acc = jnp.where(bkt == k, rel_ref[k, h], acc)
            bias_ref[0, h] = acc
            bias_ref[1, h] = jnp.where(in_cur, acc, MASK_VALUE)
        band_lane = lax.broadcasted_iota(jnp.int32, (tq + blk, LANES), 1)
        ones_lo = jnp.where(band_lane < dh, 1.0, 0.0).astype(BF16)
        for kv in range(SWA_KV_HEADS):
            vo_ref[kv, 0, :, LANES:] = ones_lo
            vo_ref[kv, 1, :, LANES:] = 1.0 - ones_lo

    def build_band(src_ref, r0, rows):
        lo = lax.broadcasted_iota(jnp.int32, (rows, LANES), 1) < dh
        zero = jnp.zeros((rows, LANES), BF16)
        for kv in range(SWA_KV_HEADS):
            c0 = LANES * (kv // 2)
            k_pair = src_ref[:, c0:c0 + LANES]
            v_pair = src_ref[:, SWA_KV_W + c0:SWA_KV_W + c0 + LANES]
            k_rot = _roll_half_lanes(k_pair)
            v_rot = _roll_half_lanes(v_pair)
            k_own, k_other, v_own, v_other = ((k_pair, k_rot, v_pair, v_rot) if kv % 2 == 0
                                              else (k_rot, k_pair, v_rot, v_pair))
            kd_ref[kv, r0:r0 + rows, :] = jnp.where(lo, k_own, k_other)
            vo_ref[kv, 0, r0:r0 + rows, :LANES] = jnp.where(lo, v_own, zero)
            vo_ref[kv, 1, r0:r0 + rows, :LANES] = jnp.where(lo, zero, v_other)

    build_band(kvp_ref, 0, blk)
    build_band(kvc_ref, blk, tq)

    lane = lax.broadcasted_iota(jnp.int32, (1, LANES), 1)
    q_scale = (jnp.where(lane < dh, dh ** -0.5, 0.0).astype(BF16), jnp.where(lane < dh, 0.0, dh ** -0.5).astype(BF16))
    zero_p = jnp.zeros((blk, blk), BF16)

    def sub_block(n, carry):
        r0 = pl.multiple_of(n * blk, blk)
        first = jnp.where(first_tile & (n == 0), 1, 0)
        scores = []
        for kv in range(SWA_KV_HEADS):
            k_band = kd_ref[kv, pl.ds(r0, 2 * blk), :]
            for pr in range(SWA_GROUP // 2):
                q_pair = q_ref[pl.ds(r0, blk), pl.ds(LANES * (2 * kv + pr), LANES)]
                for half in range(2):
                    h = SWA_GROUP * kv + 2 * pr + half
                    s = _dot_nt(q_pair * q_scale[half], k_band)
                    scores.append(jnp.where(in_cur, s[:, blk:], s[:, :blk]) + bias_ref[first, h])
        maxes = [jnp.maximum(jnp.max(s, axis=-1, keepdims=True), sink_ref[h]) for h, s in enumerate(scores)]
        exps = [jnp.exp(s - m).astype(BF16) for s, m in zip(scores, maxes)]
        for kv in range(SWA_KV_HEADS):
            for pr in range(SWA_GROUP // 2):
                h0 = SWA_GROUP * kv + 2 * pr
                acc = None
                for half in range(2):
                    e = exps[h0 + half]
                    p_band = jnp.concatenate([jnp.where(in_cur, zero_p, e), jnp.where(in_cur, e, zero_p)], axis=1)
                    part = _dot(p_band, vo_ref[kv, half, pl.ds(r0, 2 * blk), :])
                    acc = part if acc is None else acc + part
                sink_term = jnp.where(lo_half, jnp.exp(sink_ref[h0] - maxes[h0]),
                                      jnp.exp(sink_ref[h0 + 1] - maxes[h0 + 1]))
                out = acc[:, :LANES] / (acc[:, LANES:] + sink_term)
                o_ref[pl.ds(r0, blk), pl.ds(LANES * (2 * kv + pr), LANES)] = out.astype(o_ref.dtype)
        return carry

    lax.fori_loop(0, tq // blk, sub_block, 0)


def _swa_core(z, rel_bias, sinks):
    b, s, _ = z.shape
    kv_col = SWA_Q_W // (2 * SWA_KV_W)
    blocks_per_tile = SWA_TQ // SWA_BLOCK
    smem = lambda: pl.BlockSpec(memory_space=pltpu.SMEM)
    return pl.pallas_call(
        _swa_body,
        grid=(b, s // SWA_TQ),
        in_specs=[
            smem(), smem(),
            pl.BlockSpec((SWA_BLOCK, SWA_BLOCK), lambda i, j: (0, 0)),
            pl.BlockSpec((None, SWA_TQ, SWA_Q_W), lambda i, j: (i, j, 0)),
            pl.BlockSpec((None, SWA_TQ, 2 * SWA_KV_W), lambda i, j: (i, j, kv_col)),
            pl.BlockSpec((None, SWA_BLOCK, 2 * SWA_KV_W),
                         lambda i, j: (i, jnp.maximum(j * blocks_per_tile - 1, 0), kv_col)),
        ],
        out_specs=pl.BlockSpec((None, SWA_TQ, SWA_Q_W), lambda i, j: (i, j, 0)),
        out_shape=jax.ShapeDtypeStruct((b, s, SWA_Q_W), BF16),
        scratch_shapes=[
            pltpu.VMEM((2, SWA_HEADS, SWA_BLOCK, SWA_BLOCK), F32),
            pltpu.VMEM((SWA_KV_HEADS, SWA_TQ + SWA_BLOCK, LANES), BF16),
            pltpu.VMEM((SWA_KV_HEADS, 2, SWA_TQ + SWA_BLOCK, 2 * LANES), BF16),
        ],
        compiler_params=_params("arbitrary", "arbitrary"),
        name="swa_core",
    )(rel_bias, sinks, jnp.asarray(_t5_bucket_table()), z, z, z)


def _swa_mix(z, rel_bias, sinks, batch):
    t = z.shape[0]
    return _swa_core(z.reshape(batch, t // batch, SWA_Q_W + 2 * SWA_KV_W), rel_bias, sinks).reshape(t, SWA_Q_W)


def kernel(x, p, rel_bias, final_norm, ffn1_norm, ffn1_w_gate, ffn1_w_up, ffn1_w_down, mix_norm, ffn2_norm,
           ffn2_w_gate, ffn2_w_up, ffn2_w_down, ple_norm, ple_w_proj, ple_w_gate, hyb_w_in, hyb_w_out, gla_w_fup,
           gla_b_f, gla_norm, lru_conv_w, lru_conv_b, lru_w_a, lru_b_a, lru_w_x, lru_b_x, lru_lambda, swa_w_qkv,
           swa_b_qkv, swa_w_o, swa_b_o, swa_sinks):
    batch, seq, _ = x.shape
    t = batch * seq
    x = x.reshape(t, D_MODEL)
    p = p.reshape(DEPTH, t, PLE_DIM)
    gains = lambda g: g.reshape(DEPTH, 1, D_MODEL)
    ffn1 = (gains(ffn1_norm), ffn1_w_gate.astype(BF16), ffn1_w_up.astype(BF16), ffn1_w_down.astype(BF16))
    ffn2 = (gains(ffn2_norm), ffn2_w_gate.astype(BF16), ffn2_w_up.astype(BF16), ffn2_w_down.astype(BF16))
    ple = (gains(ple_norm), ple_w_gate.astype(BF16), ple_w_proj.astype(BF16))
    mix_g = gains(mix_norm)
    final_g = final_norm.reshape(1, D_MODEL)
    no_bias_in = jnp.zeros((1, HYB_ZW), F32)
    no_bias_out = jnp.zeros((1, D_MODEL), F32)
    for i in range(DEPTH):
        if i % 2 == 0:
            e = i // 2
            x, z = _pre_stage(x, i, ffn1, mix_g, _hybrid_in_weights(hyb_w_in[e]), no_bias_in, F32, "pre_hyb")
            mix = _hybrid_mix(z, gla_w_fup[e], gla_b_f[e], gla_norm[e], lru_conv_w[e], lru_conv_b[e], lru_w_a[e],
                              lru_b_a[e], lru_w_x[e], lru_b_x[e], lru_lambda[e], batch)
            x = _post_stage(mix, x, p, i, hyb_w_out[e].astype(BF16), no_bias_out, ffn2, ple, final_g, "post_hyb")
        else:
            o = i // 2
            x, z = _pre_stage(x, i, ffn1, mix_g, swa_w_qkv[o].astype(BF16), swa_b_qkv[o].reshape(1, -1), BF16,
                              "pre_swa")
            mix = _swa_mix(z, rel_bias, swa_sinks[o], batch)
            x = _post_stage(mix, x, p, i, swa_w_o[o].astype(BF16), swa_b_o[o].reshape(1, D_MODEL), ffn2, ple,
                            final_g, "post_swa")
    return x.reshape(batch, seq, D_MODEL)
```

```python
import functools
import math

import jax
import jax.numpy as jnp
import numpy as np
from jax import lax
from jax.experimental import pallas as pl
from jax.experimental.pallas import tpu as pltpu

F32 = jnp.float32
BF16 = jnp.bfloat16

D_MODEL = 1024
DEPTH = 4
PLE_DIM = 256
D_FF = 2816
EPS = 1e-6

GLA_HEADS = 4
GLA_DK = 64
GLA_DV = 128
GLA_KW = GLA_HEADS * GLA_DK
GLA_VW = GLA_HEADS * GLA_DV
GLA_GATE_RANK = 16
GLA_GATE_TAU = 16.0
GLA_CHUNK = 64
LRU_WIDTH = 512
LRU_BLOCKS = 8
LRU_CONV_W = 4
LRU_C = 8.0

SWA_HEADS = 16
SWA_KV_HEADS = 4
SWA_HEAD_DIM = 64
SWA_GROUP = SWA_HEADS // SWA_KV_HEADS
SWA_BLOCK = 128
SWA_Q_W = SWA_HEADS * SWA_HEAD_DIM
SWA_KV_W = SWA_KV_HEADS * SWA_HEAD_DIM
REL_BUCKETS = 32
REL_MAX_DIST = 128
MASK_VALUE = -1e30

LANES = 128
SUBLANES = 8
VMEM_LIMIT_BYTES = 56 * 1024 * 1024

TOK_TM = 512
FFN_TF = 256
HYB_TS = 256
LRU_ROW_BLOCK = 64
SWA_TQ = 512

Z_Q = 0
Z_K = Z_Q + GLA_KW
Z_V = Z_K + GLA_KW
Z_R = Z_V + GLA_VW
Z_X = Z_R + GLA_VW
Z_G = Z_X + LRU_WIDTH
Z_F = Z_G + LRU_WIDTH
HYB_ZW = Z_F + LANES


def _params(*semantics):
    return pltpu.CompilerParams(dimension_semantics=semantics, vmem_limit_bytes=VMEM_LIMIT_BYTES)


def _rms(x, g):
    return x * lax.rsqrt(jnp.mean(x * x, axis=-1, keepdims=True) + EPS) * g


def _sigmoid(x):
    return 1.0 / (1.0 + jnp.exp(-x))


def _dot(a, b):
    return jnp.dot(a, b, preferred_element_type=F32)


def _dot_nt(a, b):
    return lax.dot_general(a, b, (((1,), (1,)), ((), ())), preferred_element_type=F32)


def _dot_tn(a, b):
    return lax.dot_general(a, b, (((0,), (0,)), ((), ())), preferred_element_type=F32)


def _ffn_apply(x, g_ref, wg_ref, wu_ref, wd_ref, fillers=()):
    xn = _rms(x, g_ref[...]).astype(BF16)
    acc = x
    for idx, c0 in enumerate(range(0, D_FF, FFN_TF)):
        a = _dot(xn, wg_ref[:, c0:c0 + FFN_TF])
        b = _dot(xn, wu_ref[:, c0:c0 + FFN_TF])
        h = (0.5 * a) * _sigmoid(a) * b
        acc = acc + _dot(h.astype(BF16), wd_ref[c0:c0 + FFN_TF, :])
        if idx < len(fillers):
            fillers[idx]()
    return acc


def _pre_body(x_ref, g1_ref, wg_ref, wu_ref, wd_ref, gm_ref, win_ref, bin_ref, xo_ref, z_ref):
    x = _ffn_apply(x_ref[...], g1_ref, wg_ref, wu_ref, wd_ref)
    xo_ref[...] = x
    z_ref[...] = (_dot(_rms(x, gm_ref[...]).astype(BF16), win_ref[...]) + bin_ref[...]).astype(z_ref.dtype)


def _post_tail(x, p_ref, g2_ref, wg_ref, wu_ref, wd_ref, gp_ref, wpg_ref, wpp_ref, fg_ref, o_ref, final, fillers=()):
    x = _ffn_apply(x, g2_ref, wg_ref, wu_ref, wd_ref, fillers)
    gate = _sigmoid(_dot(_rms(x, gp_ref[...]).astype(BF16), wpg_ref[...]))
    x = x + gate * _dot(p_ref[...].astype(BF16), wpp_ref[...])
    if final:
        x = _rms(x, fg_ref[...])
    o_ref[...] = x


def _post_body(mix_ref, x_ref, p_ref, wo_ref, bo_ref, g2_ref, wg_ref, wu_ref, wd_ref, gp_ref, wpg_ref, wpp_ref,
               fg_ref, o_ref, *, final):
    x = x_ref[...] + (_dot(mix_ref[...], wo_ref[...]) + bo_ref[...])
    _post_tail(x, p_ref, g2_ref, wg_ref, wu_ref, wd_ref, gp_ref, wpg_ref, wpp_ref, fg_ref, o_ref, final)


def _post_hyb_body(gla_ref, xr0_ref, gr0_ref, xr_ref, gr_ref, x_ref, p_ref, wo_ref, cw_ref, cb_ref, wa_ref, ba_ref,
                   wx_ref, bx_ref, lam_ref, g2_ref, wg_ref, wu_ref, wd_ref, gp_ref, wpg_ref, wpp_ref, fg_ref, o_ref,
                   lru_ref, xbuf_ref, hc_ref, *, final, tiles_per_seq):
    lru_args = (cw_ref, cb_ref, wa_ref, ba_ref, wx_ref, bx_ref, lam_ref, xbuf_ref, hc_ref)
    step = pl.program_id(0)

    def reset_state():
        xbuf_ref[0:SUBLANES, :] = jnp.zeros((SUBLANES, LRU_WIDTH), F32)
        hc_ref[...] = jnp.zeros_like(hc_ref)

    @pl.when(step == 0)
    def _():
        reset_state()
        for stage in _lru_stages(xr0_ref, gr0_ref, *lru_args, lru_ref):
            stage()

    @pl.when((step + 1) % tiles_per_seq == 0)
    def _():
        reset_state()

    x = x_ref[...] + (_dot(gla_ref[...], wo_ref[0:GLA_VW, :])
                      + _dot(lru_ref[...], wo_ref[GLA_VW:GLA_VW + LRU_WIDTH, :]))
    prepare, *row_blocks = _lru_stages(xr_ref, gr_ref, *lru_args, lru_ref)
    prepare()
    _post_tail(x, p_ref, g2_ref, wg_ref, wu_ref, wd_ref, gp_ref, wpg_ref, wpp_ref, fg_ref, o_ref, final, row_blocks)


def _rows(width):
    return pl.BlockSpec((TOK_TM, width), lambda i: (i, 0))


def _resident(array, layer=None):
    if layer is None:
        return pl.BlockSpec(array.shape, lambda i: (0, 0), pipeline_mode=pl.Buffered(1))
    return pl.BlockSpec((None,) + array.shape[1:], lambda i: (layer, 0, 0), pipeline_mode=pl.Buffered(1))


def _pre_stage(x, layer, ffn, mix_g, w_in, b_in, z_dtype, name):
    t = x.shape[0]
    n = w_in.shape[1]
    return pl.pallas_call(
        _pre_body,
        grid=(t // TOK_TM,),
        in_specs=[_rows(D_MODEL)] + [_resident(w, layer) for w in ffn]
        + [_resident(mix_g, layer), _resident(w_in), _resident(b_in)],
        out_specs=[_rows(D_MODEL), _rows(n)],
        out_shape=[jax.ShapeDtypeStruct((t, D_MODEL), F32), jax.ShapeDtypeStruct((t, n), z_dtype)],
        compiler_params=_params("parallel"),
        name=name,
    )(x, *ffn, mix_g, w_in, b_in)


def _post_stage(mix, x, p, layer, w_o, b_o, ffn, ple, final_g, name):
    t = x.shape[0]
    return pl.pallas_call(
        functools.partial(_post_body, final=(layer == DEPTH - 1)),
        grid=(t // TOK_TM,),
        in_specs=[_rows(mix.shape[1]), _rows(D_MODEL),
                  pl.BlockSpec((None, TOK_TM, PLE_DIM), lambda i: (layer, i, 0)),
                  _resident(w_o), _resident(b_o)]
        + [_resident(w, layer) for w in ffn] + [_resident(w, layer) for w in ple] + [_resident(final_g)],
        out_specs=_rows(D_MODEL),
        out_shape=jax.ShapeDtypeStruct((t, D_MODEL), F32),
        compiler_params=_params("parallel"),
        name=name,
    )(mix, x, p, w_o, b_o, *ffn, *ple, final_g)


def _post_hyb_stage(gla, z, x, p, layer, w_o, lru, ffn, ple, final_g, seq):
    t = x.shape[0]
    last = t // TOK_TM - 1
    first_cols = lambda block: pl.BlockSpec((TOK_TM, LRU_WIDTH), lambda i: (0, block))
    next_cols = lambda block: pl.BlockSpec((TOK_TM, LRU_WIDTH), lambda i: (jnp.minimum(i + 1, last), block))
    return pl.pallas_call(
        functools.partial(_post_hyb_body, final=(layer == DEPTH - 1), tiles_per_seq=seq // TOK_TM),
        grid=(t // TOK_TM,),
        in_specs=[_rows(GLA_VW), first_cols(Z_X // LRU_WIDTH), first_cols(Z_G // LRU_WIDTH),
                  next_cols(Z_X // LRU_WIDTH), next_cols(Z_G // LRU_WIDTH), _rows(D_MODEL),
                  pl.BlockSpec((None, TOK_TM, PLE_DIM), lambda i: (layer, i, 0)), _resident(w_o)]
        + [pl.BlockSpec(w.shape, lambda i, n=w.ndim: (0,) * n, pipeline_mode=pl.Buffered(1)) for w in lru]
        + [_resident(w, layer) for w in ffn] + [_resident(w, layer) for w in ple] + [_resident(final_g)],
        out_specs=_rows(D_MODEL),
        out_shape=jax.ShapeDtypeStruct((t, D_MODEL), F32),
        scratch_shapes=[
            pltpu.VMEM((TOK_TM, LRU_WIDTH), BF16),
            pltpu.VMEM((TOK_TM + 2 * SUBLANES, LRU_WIDTH), F32),
            pltpu.VMEM((1, LRU_WIDTH), F32),
        ],
        compiler_params=_params("arbitrary"),
        name="post_hyb",
    )(gla, z, z, z, z, x, p, w_o, *lru, *ffn, *ple, final_g)


def _lru_stages(xr_ref, gr_ref, cw_ref, cb_ref, wa_ref, ba_ref, wx_ref, bx_ref, lam_ref, xbuf_ref, hc_ref, out_ref):
    rows = xr_ref.shape[0]
    half_w = LRU_WIDTH // 2
    taps_back = LRU_CONV_W - 1
    vals = {}

    def prepare():
        xbuf_ref[SUBLANES:SUBLANES + rows, :] = xr_ref[...]
        xc = xbuf_ref[SUBLANES - taps_back:SUBLANES - taps_back + rows, :] * cw_ref[0:1, :]
        for tap in range(1, LRU_CONV_W):
            off = SUBLANES - taps_back + tap
            xc = xc + xbuf_ref[off:off + rows, :] * cw_ref[tap:tap + 1, :]
        xc = xc + cb_ref[...]
        xbuf_ref[0:SUBLANES, :] = xbuf_ref[rows:rows + SUBLANES, :]
        xcb = xc.astype(BF16)
        vals["r_pre"] = [_dot(xcb[:, d * half_w:(d + 1) * half_w], wa_ref[d]) for d in range(2)]
        vals["i_pre"] = [_dot(xcb[:, d * half_w:(d + 1) * half_w], wx_ref[d]) for d in range(2)]
        neg_lam = -lam_ref[...]
        vals["softplus"] = jnp.maximum(neg_lam, 0.0) + jnp.log1p(jnp.exp(-jnp.abs(neg_lam)))
        vals["xc"] = xc
        vals["carry"] = hc_ref[...]

    def row_block(r0):
        rs = slice(r0, r0 + LRU_ROW_BLOCK)
        r_gate = _sigmoid(jnp.concatenate([half[rs] for half in vals["r_pre"]], axis=1) + ba_ref[...])
        i_gate = _sigmoid(jnp.concatenate([half[rs] for half in vals["i_pre"]], axis=1) + bx_ref[...])
        log_a = -LRU_C * r_gate * vals["softplus"]
        a_all = jnp.exp(log_a)
        th = jnp.tanh(log_a)
        u_all = jnp.sqrt(-2.0 * th / (1.0 - th)) * (i_gate * vals["xc"][rs])
        srow = lax.broadcasted_iota(jnp.int32, (SUBLANES, LRU_WIDTH), 0)
        carry = vals["carry"]
        h_groups = []
        for g in range(LRU_ROW_BLOCK // SUBLANES):
            a = a_all[g * SUBLANES:(g + 1) * SUBLANES]
            u = u_all[g * SUBLANES:(g + 1) * SUBLANES]
            for s in (1, 2, 4):
                keep = srow >= s
                a_sh = jnp.where(keep, pltpu.roll(a, s, 0), 1.0)
                u_sh = jnp.where(keep, pltpu.roll(u, s, 0), 0.0)
                u = a * u_sh + u
                a = a * a_sh
            h_groups.append(a * carry + u)
            carry = a[SUBLANES - 1:SUBLANES, :] * carry + u[SUBLANES - 1:SUBLANES, :]
        vals["carry"] = carry
        if r0 + LRU_ROW_BLOCK == rows:
            hc_ref[...] = carry
        g_in = gr_ref[rs, :]
        gelu = 0.5 * g_in * (1.0 + jnp.tanh(math.sqrt(2.0 / math.pi) * (g_in + 0.044715 * (g_in * g_in * g_in))))
        out_ref[rs, :] = (jnp.concatenate(h_groups, axis=0) * gelu).astype(out_ref.dtype)

    return [prepare] + [functools.partial(row_block, r0) for r0 in range(0, rows, LRU_ROW_BLOCK)]


def _split3(x):
    hi = x.astype(BF16)
    r1 = x - hi.astype(F32)
    mid = r1.astype(BF16)
    lo = (r1 - mid.astype(F32)).astype(BF16)
    return hi, mid, lo


def _gla_body(z_ref, wf_ref, bf_ref, gn_ref, o_ref, st_ref):
    ts = HYB_TS
    c_len = GLA_CHUNK
    n_chunks = ts // c_len

    @pl.when(pl.program_id(1) == 0)
    def _():
        st_ref[...] = jnp.zeros_like(st_ref)

    f_low = z_ref[:, Z_F:Z_F + LANES].astype(BF16)
    gate_in = _dot(f_low, wf_ref[...]) + bf_ref[...]
    log_f = (jnp.minimum(gate_in, 0.0) - jnp.log1p(jnp.exp(-jnp.abs(gate_in)))) * (1.0 / GLA_GATE_TAU)
    row = lax.broadcasted_iota(jnp.int32, (ts, ts), 0)
    col = lax.broadcasted_iota(jnp.int32, (ts, ts), 1)
    in_chunk_causal = (row // c_len == col // c_len) & (col <= row)
    tri = jnp.where(in_chunk_causal, 1.0, 0.0).astype(BF16)
    hi, mid, lo = _split3(log_f)
    b_all = _dot(tri, hi) + _dot(tri, mid) + _dot(tri, lo)
    b_last = [b_all[(c + 1) * c_len - 1:(c + 1) * c_len] for c in range(n_chunks)]
    b_last_rows = jnp.concatenate([jnp.broadcast_to(bl, (c_len, GLA_KW)) for bl in b_last], axis=0)
    decay = [jnp.exp(bl) for bl in b_last]
    q_all = z_ref[:, Z_Q:Z_Q + GLA_KW]
    k_all = z_ref[:, Z_K:Z_K + GLA_KW]
    q_dec = (q_all * (GLA_DK ** -0.5) * jnp.exp(b_all)).astype(BF16)
    k_dec = (k_all * jnp.exp(-b_all)).astype(BF16)
    k_end = (k_all * jnp.exp(b_last_rows - b_all)).astype(BF16)
    lane = lax.broadcasted_iota(jnp.int32, (1, LANES), 1)
    head_lanes = (jnp.where(lane < GLA_DK, 1.0, 0.0).astype(BF16), jnp.where(lane < GLA_DK, 0.0, 1.0).astype(BF16))
    row_chunk = lax.broadcasted_iota(jnp.int32, (ts, LANES), 0) // c_len
    zero_k = jnp.zeros((ts, LANES), BF16)

    for pair in range(GLA_HEADS // 2):
        kl = slice(pair * LANES, (pair + 1) * LANES)
        q_pair = q_dec[:, kl]
        q_heads = jnp.concatenate([q_pair * head_lanes[0], q_pair * head_lanes[1]], axis=0)
        scores = _dot_nt(q_heads, k_dec[:, kl])
        k_end_pair = k_end[:, kl]
        k_end_blocks = jnp.concatenate(
            [jnp.where(row_chunk == c, k_end_pair, zero_k) for c in range(n_chunks)], axis=1)
        for half in range(2):
            hd = 2 * pair + half
            vl = slice(hd * GLA_DV, (hd + 1) * GLA_DV)
            q_h = q_heads[half * ts:(half + 1) * ts]
            att = jnp.where(in_chunk_causal, scores[half * ts:(half + 1) * ts], 0.0).astype(BF16)
            v_h = z_ref[:, Z_V + hd * GLA_DV:Z_V + (hd + 1) * GLA_DV].astype(BF16)
            o = _dot(att, v_h)
            kv_t = _dot_tn(v_h, k_end_blocks)
            s_t = st_ref[hd]
            o_inter = []
            for c in range(n_chunks):
                o_inter.append(_dot_nt(q_h[c * c_len:(c + 1) * c_len], s_t.astype(BF16)))
                s_t = s_t * decay[c][:, kl] + kv_t[:, c * LANES:(c + 1) * LANES]
            st_ref[hd] = s_t
            o = o + jnp.concatenate(o_inter, axis=0)
            o = o * lax.rsqrt(jnp.mean(o * o, axis=-1, keepdims=True) + EPS)
            r_h = z_ref[:, Z_R + hd * GLA_DV:Z_R + (hd + 1) * GLA_DV]
            o_ref[:, vl] = (o * gn_ref[:, vl] * (r_h * _sigmoid(r_h))).astype(o_ref.dtype)


def _gla_core(z, wf, bf, gn):
    b, s, _ = z.shape
    const = lambda shape: pl.BlockSpec(shape, lambda i, j: (0,) * len(shape))
    return pl.pallas_call(
        _gla_body,
        grid=(b, s // HYB_TS),
        in_specs=[
            pl.BlockSpec((None, HYB_TS, HYB_ZW), lambda i, j: (i, j, 0)),
            const((LANES, GLA_KW)), const((1, GLA_KW)), const((1, GLA_VW)),
        ],
        out_specs=pl.BlockSpec((None, HYB_TS, GLA_VW), lambda i, j: (i, j, 0)),
        out_shape=jax.ShapeDtypeStruct((b, s, GLA_VW), BF16),
        scratch_shapes=[pltpu.VMEM((GLA_HEADS, GLA_DV, LANES), F32)],
        compiler_params=_params("parallel", "arbitrary"),
        name="gla_core",
    )(z, wf, bf, gn)


def _block_diag_halves(w):
    g, bw, _ = w.shape
    eye = jnp.eye(g // 2, dtype=w.dtype)
    w = w.reshape(2, g // 2, bw, bw)
    return (eye[None, :, None, :, None] * w[:, :, :, None, :]).reshape(2, g * bw // 2, g * bw // 2)


def _hybrid_in_weights(w_in):
    q_w, k_w, v_w, r_w, f_w, xr_w, gr_w = jnp.split(
        w_in, np.cumsum([GLA_KW, GLA_KW, GLA_VW, GLA_VW, GLA_GATE_RANK, LRU_WIDTH]).tolist(), axis=1)
    f_w = jnp.pad(f_w, ((0, 0), (0, LANES - GLA_GATE_RANK)))
    return jnp.concatenate([q_w, k_w, v_w, r_w, xr_w, gr_w, f_w], axis=1).astype(BF16)


def _gla_mix(z, w_fup, b_f, gla_norm, batch):
    t = z.shape[0]
    wf = jnp.pad(w_fup, ((0, LANES - GLA_GATE_RANK), (0, 0))).astype(BF16)
    gla = _gla_core(z.reshape(batch, t // batch, HYB_ZW), wf, b_f.reshape(1, GLA_KW), gla_norm.reshape(1, GLA_VW))
    return gla.reshape(t, GLA_VW)


def _t5_bucket_table():
    max_exact = REL_BUCKETS // 2
    dist = np.arange(SWA_BLOCK)
    d = np.maximum(dist, 1).astype(np.float32)
    large = max_exact + (np.log(d / max_exact) / math.log(REL_MAX_DIST / max_exact)
                         * (REL_BUCKETS - max_exact)).astype(np.int32)
    bucket = np.where(dist < max_exact, dist, np.minimum(large, REL_BUCKETS - 1)).astype(np.int32)
    i = np.arange(SWA_BLOCK)[:, None]
    j = np.arange(SWA_BLOCK)[None, :]
    return bucket[(i - j) % SWA_BLOCK]


def _roll_half_lanes(x):
    words = pltpu.bitcast(x, jnp.int32)
    return pltpu.bitcast(pltpu.roll(words, LANES // 2, 1), BF16)


def _swa_body(rel_ref, sink_ref, bkt_ref, q_ref, kvc_ref, kvp_ref, o_ref, bias_ref, kd_ref, vo_ref):
    blk = SWA_BLOCK
    dh = SWA_HEAD_DIM
    tq = SWA_TQ
    first_tile = pl.program_id(1) == 0
    row = lax.broadcasted_iota(jnp.int32, (blk, blk), 0)
    col = lax.broadcasted_iota(jnp.int32, (blk, blk), 1)
    in_cur = col <= row
    lo_half = col < dh

    @pl.when((pl.program_id(0) == 0) & first_tile)
    def _():
        bkt = bkt_ref[...]
        for h in range(SWA_HEADS):
            acc = jnp.zeros((blk, blk), F32)
            for k in range(REL_BUCKETS):
                acc = jnp.where(bkt == k, rel_ref[k, h], acc)
            bias_ref[0, h] = acc
            bias_ref[1, h] = jnp.where(in_cur, acc, MASK_VALUE)
        band_lane = lax.broadcasted_iota(jnp.int32, (tq + blk, LANES), 1)
        ones_lo = jnp.where(band_lane < dh, 1.0, 0.0).astype(BF16)
        for kv in range(SWA_KV_HEADS):
            vo_ref[kv, 0, :, LANES:] = ones_lo
            vo_ref[kv, 1, :, LANES:] = 1.0 - ones_lo

    def build_band(src_ref, r0, rows):
        lo = lax.broadcasted_iota(jnp.int32, (rows, LANES), 1) < dh
        zero = jnp.zeros((rows, LANES), BF16)
        for kv in range(SWA_KV_HEADS):
            c0 = LANES * (kv // 2)
            k_pair = src_ref[:, c0:c0 + LANES]
            v_pair = src_ref[:, SWA_KV_W + c0:SWA_KV_W + c0 + LANES]
            k_rot = _roll_half_lanes(k_pair)
            v_rot = _roll_half_lanes(v_pair)
            k_own, k_other, v_own, v_other = ((k_pair, k_rot, v_pair, v_rot) if kv % 2 == 0
                                              else (k_rot, k_pair, v_rot, v_pair))
            kd_ref[kv, r0:r0 + rows, :] = jnp.where(lo, k_own, k_other)
            vo_ref[kv, 0, r0:r0 + rows, :LANES] = jnp.where(lo, v_own, zero)
            vo_ref[kv, 1, r0:r0 + rows, :LANES] = jnp.where(lo, zero, v_other)

    build_band(kvp_ref, 0, blk)
    build_band(kvc_ref, blk, tq)

    lane = lax.broadcasted_iota(jnp.int32, (1, LANES), 1)
    q_scale = (jnp.where(lane < dh, dh ** -0.5, 0.0).astype(BF16), jnp.where(lane < dh, 0.0, dh ** -0.5).astype(BF16))
    zero_p = jnp.zeros((blk, blk), BF16)

    def sub_block(n, carry):
        r0 = pl.multiple_of(n * blk, blk)
        first = jnp.where(first_tile & (n == 0), 1, 0)
        scores = []
        for kv in range(SWA_KV_HEADS):
            k_band = kd_ref[kv, pl.ds(r0, 2 * blk), :]
            for pr in range(SWA_GROUP // 2):
                q_pair = q_ref[pl.ds(r0, blk), pl.ds(LANES * (2 * kv + pr), LANES)]
                for half in range(2):
                    h = SWA_GROUP * kv + 2 * pr + half
                    s = _dot_nt(q_pair * q_scale[half], k_band)
                    scores.append(jnp.where(in_cur, s[:, blk:], s[:, :blk]) + bias_ref[first, h])
        maxes = [jnp.maximum(jnp.max(s, axis=-1, keepdims=True), sink_ref[h]) for h, s in enumerate(scores)]
        exps = [jnp.exp(s - m).astype(BF16) for s, m in zip(scores, maxes)]
        for kv in range(SWA_KV_HEADS):
            for pr in range(SWA_GROUP // 2):
                h0 = SWA_GROUP * kv + 2 * pr
                acc = None
                for half in range(2):
                    e = exps[h0 + half]
                    p_band = jnp.concatenate([jnp.where(in_cur, zero_p, e), jnp.where(in_cur, e, zero_p)], axis=1)
                    part = _dot(p_band, vo_ref[kv, half, pl.ds(r0, 2 * blk), :])
                    acc = part if acc is None else acc + part
                sink_term = jnp.where(lo_half, jnp.exp(sink_ref[h0] - maxes[h0]),
                                      jnp.exp(sink_ref[h0 + 1] - maxes[h0 + 1]))
                out = acc[:, :LANES] / (acc[:, LANES:] + sink_term)
                o_ref[pl.ds(r0, blk), pl.ds(LANES * (2 * kv + pr), LANES)] = out.astype(o_ref.dtype)
        return carry

    lax.fori_loop(0, tq // blk, sub_block, 0)


def _swa_core(z, rel_bias, sinks):
    b, s, _ = z.shape
    kv_col = SWA_Q_W // (2 * SWA_KV_W)
    blocks_per_tile = SWA_TQ // SWA_BLOCK
    smem = lambda: pl.BlockSpec(memory_space=pltpu.SMEM)
    return pl.pallas_call(
        _swa_body,
        grid=(b, s // SWA_TQ),
        in_specs=[
            smem(), smem(),
            pl.BlockSpec((SWA_BLOCK, SWA_BLOCK), lambda i, j: (0, 0)),
            pl.BlockSpec((None, SWA_TQ, SWA_Q_W), lambda i, j: (i, j, 0)),
            pl.BlockSpec((None, SWA_TQ, 2 * SWA_KV_W), lambda i, j: (i, j, kv_col)),
            pl.BlockSpec((None, SWA_BLOCK, 2 * SWA_KV_W),
                         lambda i, j: (i, jnp.maximum(j * blocks_per_tile - 1, 0), kv_col)),
        ],
        out_specs=pl.BlockSpec((None, SWA_TQ, SWA_Q_W), lambda i, j: (i, j, 0)),
        out_shape=jax.ShapeDtypeStruct((b, s, SWA_Q_W), BF16),
        scratch_shapes=[
            pltpu.VMEM((2, SWA_HEADS, SWA_BLOCK, SWA_BLOCK), F32),
            pltpu.VMEM((SWA_KV_HEADS, SWA_TQ + SWA_BLOCK, LANES), BF16),
            pltpu.VMEM((SWA_KV_HEADS, 2, SWA_TQ + SWA_BLOCK, 2 * LANES), BF16),
        ],
        compiler_params=_params("arbitrary", "arbitrary"),
        name="swa_core",
    )(rel_bias, sinks, jnp.asarray(_t5_bucket_table()), z, z, z)


def _swa_mix(z, rel_bias, sinks, batch):
    t = z.shape[0]
    return _swa_core(z.reshape(batch, t // batch, SWA_Q_W + 2 * SWA_KV_W), rel_bias, sinks).reshape(t, SWA_Q_W)


def kernel(x, p, rel_bias, final_norm, ffn1_norm, ffn1_w_gate, ffn1_w_up, ffn1_w_down, mix_norm, ffn2_norm,
           ffn2_w_gate, ffn2_w_up, ffn2_w_down, ple_norm, ple_w_proj, ple_w_gate, hyb_w_in, hyb_w_out, gla_w_fup,
           gla_b_f, gla_norm, lru_conv_w, lru_conv_b, lru_w_a, lru_b_a, lru_w_x, lru_b_x, lru_lambda, swa_w_qkv,
           swa_b_qkv, swa_w_o, swa_b_o, swa_sinks):
    batch, seq, _ = x.shape
    t = batch * seq
    x = x.reshape(t, D_MODEL)
    p = p.reshape(DEPTH, t, PLE_DIM)
    gains = lambda g: g.reshape(DEPTH, 1, D_MODEL)
    ffn1 = (gains(ffn1_norm), ffn1_w_gate.astype(BF16), ffn1_w_up.astype(BF16), ffn1_w_down.astype(BF16))
    ffn2 = (gains(ffn2_norm), ffn2_w_gate.astype(BF16), ffn2_w_up.astype(BF16), ffn2_w_down.astype(BF16))
    ple = (gains(ple_norm), ple_w_gate.astype(BF16), ple_w_proj.astype(BF16))
    mix_g = gains(mix_norm)
    final_g = final_norm.reshape(1, D_MODEL)
    no_bias_in = jnp.zeros((1, HYB_ZW), F32)
    for i in range(DEPTH):
        if i % 2 == 0:
            e = i // 2
            x, z = _pre_stage(x, i, ffn1, mix_g, _hybrid_in_weights(hyb_w_in[e]), no_bias_in, F32, "pre_hyb")
            gla = _gla_mix(z, gla_w_fup[e], gla_b_f[e], gla_norm[e], batch)
            lru = (lru_conv_w[e], lru_conv_b[e].reshape(1, LRU_WIDTH),
                   _block_diag_halves(lru_w_a[e]).astype(BF16), lru_b_a[e].reshape(1, LRU_WIDTH),
                   _block_diag_halves(lru_w_x[e]).astype(BF16), lru_b_x[e].reshape(1, LRU_WIDTH),
                   lru_lambda[e].reshape(1, LRU_WIDTH))
            x = _post_hyb_stage(gla, z, x, p, i, hyb_w_out[e].astype(BF16), lru, ffn2, ple, final_g, seq)
        else:
            o = i // 2
            x, z = _pre_stage(x, i, ffn1, mix_g, swa_w_qkv[o].astype(BF16), swa_b_qkv[o].reshape(1, -1), BF16,
                              "pre_swa")
            mix = _swa_mix(z, rel_bias, swa_sinks[o], batch)
            x = _post_stage(mix, x, p, i, swa_w_o[o].astype(BF16), swa_b_o[o].reshape(1, D_MODEL), ffn2, ple,
                            final_g, "post_swa")
    return x.reshape(batch, seq, D_MODEL)
```

```python
import functools
import math

import jax
import jax.numpy as jnp
import numpy as np
from jax import lax
from jax.experimental import pallas as pl
from jax.experimental.pallas import tpu as pltpu

F32 = jnp.float32
BF16 = jnp.bfloat16

D_MODEL = 1024
DEPTH = 4
PLE_DIM = 256
D_FF = 2816
EPS = 1e-6

GLA_HEADS = 4
GLA_DK = 64
GLA_DV = 128
GLA_KW = GLA_HEADS * GLA_DK
GLA_VW = GLA_HEADS * GLA_DV
GLA_GATE_RANK = 16
GLA_GATE_TAU = 16.0
GLA_CHUNK = 64
LRU_WIDTH = 512
LRU_BLOCKS = 8
LRU_CONV_W = 4
LRU_C = 8.0

SWA_HEADS = 16
SWA_KV_HEADS = 4
SWA_HEAD_DIM = 64
SWA_GROUP = SWA_HEADS // SWA_KV_HEADS
SWA_BLOCK = 128
SWA_Q_W = SWA_HEADS * SWA_HEAD_DIM
SWA_KV_W = SWA_KV_HEADS * SWA_HEAD_DIM
REL_BUCKETS = 32
REL_MAX_DIST = 128
MASK_VALUE = -1e30

LANES = 128
SUBLANES = 8
VMEM_LIMIT_BYTES = 56 * 1024 * 1024

TOK_TM = 1024
PRE_HYB_TM = 512
FFN_TF = 256
HYB_TS = 256
SWA_TQ = 512

Z_Q = 0
Z_K = Z_Q + GLA_KW
Z_V = Z_K + GLA_KW
Z_R = Z_V + GLA_VW
Z_X = Z_R + GLA_VW
Z_G = Z_X + LRU_WIDTH
Z_F = Z_G + LRU_WIDTH
HYB_ZW = Z_F + LANES


def _params(*semantics):
    return pltpu.CompilerParams(dimension_semantics=semantics, vmem_limit_bytes=VMEM_LIMIT_BYTES)


def _rms(x, g):
    return x * lax.rsqrt(jnp.mean(x * x, axis=-1, keepdims=True) + EPS) * g


def _sigmoid(x):
    return 1.0 / (1.0 + jnp.exp(-x))


def _dot(a, b):
    return jnp.dot(a, b, preferred_element_type=F32)


def _dot_nt(a, b):
    return lax.dot_general(a, b, (((1,), (1,)), ((), ())), preferred_element_type=F32)


def _dot_tn(a, b):
    return lax.dot_general(a, b, (((0,), (0,)), ((), ())), preferred_element_type=F32)


def _ffn_apply(x, g_ref, wg_ref, wu_ref, wd_ref):
    xn = _rms(x, g_ref[...]).astype(BF16)
    acc = x
    starts = range(0, D_FF, FFN_TF)
    gate_up = lambda c0: (_dot(xn, wg_ref[:, c0:c0 + FFN_TF]), _dot(xn, wu_ref[:, c0:c0 + FFN_TF]))
    nxt = gate_up(starts[0])
    for idx, c0 in enumerate(starts):
        a, b = nxt
        if idx + 1 < len(starts):
            nxt = gate_up(starts[idx + 1])
        h = (0.5 * a) * _sigmoid(a) * b
        acc = acc + _dot(h.astype(BF16), wd_ref[c0:c0 + FFN_TF, :])
    return acc


def _pre_body(x_ref, g1_ref, wg_ref, wu_ref, wd_ref, gm_ref, win_ref, bin_ref, xo_ref, z_ref):
    x = _ffn_apply(x_ref[...], g1_ref, wg_ref, wu_ref, wd_ref)
    xo_ref[...] = x
    z_ref[...] = (_dot(_rms(x, gm_ref[...]).astype(BF16), win_ref[...]) + bin_ref[...]).astype(z_ref.dtype)


def _post_body(mix_ref, x_ref, p_ref, wo_ref, bo_ref, g2_ref, wg_ref, wu_ref, wd_ref, gp_ref, wpg_ref, wpp_ref,
               fg_ref, o_ref, *, final):
    x = x_ref[...] + (_dot(mix_ref[...], wo_ref[...]) + bo_ref[...])
    x = _ffn_apply(x, g2_ref, wg_ref, wu_ref, wd_ref)
    gate = _sigmoid(_dot(_rms(x, gp_ref[...]).astype(BF16), wpg_ref[...]))
    x = x + gate * _dot(p_ref[...].astype(BF16), wpp_ref[...])
    if final:
        x = _rms(x, fg_ref[...])
    o_ref[...] = x


def _rows(width, tm=TOK_TM):
    return pl.BlockSpec((tm, width), lambda i: (i, 0))


def _resident(array, layer=None):
    if layer is None:
        return pl.BlockSpec(array.shape, lambda i: (0, 0), pipeline_mode=pl.Buffered(1))
    return pl.BlockSpec((None,) + array.shape[1:], lambda i: (layer, 0, 0), pipeline_mode=pl.Buffered(1))


def _pre_stage(x, layer, ffn, mix_g, w_in, b_in, z_dtype, tm, name):
    t = x.shape[0]
    n = w_in.shape[1]
    return pl.pallas_call(
        _pre_body,
        grid=(t // tm,),
        in_specs=[_rows(D_MODEL, tm)] + [_resident(w, layer) for w in ffn]
        + [_resident(mix_g, layer), _resident(w_in), _resident(b_in)],
        out_specs=[_rows(D_MODEL, tm), _rows(n, tm)],
        out_shape=[jax.ShapeDtypeStruct((t, D_MODEL), F32), jax.ShapeDtypeStruct((t, n), z_dtype)],
        compiler_params=_params("parallel"),
        name=name,
    )(x, *ffn, mix_g, w_in, b_in)


def _post_stage(mix, x, p, layer, w_o, b_o, ffn, ple, final_g, name):
    t = x.shape[0]
    return pl.pallas_call(
        functools.partial(_post_body, final=(layer == DEPTH - 1)),
        grid=(t // TOK_TM,),
        in_specs=[_rows(mix.shape[1]), _rows(D_MODEL),
                  pl.BlockSpec((None, TOK_TM, PLE_DIM), lambda i: (layer, i, 0)),
                  _resident(w_o), _resident(b_o)]
        + [_resident(w, layer) for w in ffn] + [_resident(w, layer) for w in ple] + [_resident(final_g)],
        out_specs=_rows(D_MODEL),
        out_shape=jax.ShapeDtypeStruct((t, D_MODEL), F32),
        compiler_params=_params("parallel"),
        name=name,
    )(mix, x, p, w_o, b_o, *ffn, *ple, final_g)


def _split3(x):
    hi = x.astype(BF16)
    r1 = x - hi.astype(F32)
    mid = r1.astype(BF16)
    lo = (r1 - mid.astype(F32)).astype(BF16)
    return hi, mid, lo


def _hybrid_body(z_ref, wf_ref, bf_ref, gn_ref, cw_ref, cb_ref, wa_ref, ba_ref, wx_ref, bx_ref, lam_ref,
                 o_ref, st_ref, xbuf_ref, hc_ref):
    ts = HYB_TS
    c_len = GLA_CHUNK
    n_chunks = ts // c_len
    half_w = LRU_WIDTH // 2

    @pl.when(pl.program_id(1) == 0)
    def _():
        st_ref[...] = jnp.zeros_like(st_ref)
        xbuf_ref[0:SUBLANES, :] = jnp.zeros((SUBLANES, LRU_WIDTH), F32)
        hc_ref[...] = jnp.zeros_like(hc_ref)

    xbuf_ref[SUBLANES:SUBLANES + ts, :] = z_ref[:, Z_X:Z_X + LRU_WIDTH]
    taps_back = LRU_CONV_W - 1
    xc = xbuf_ref[SUBLANES - taps_back:SUBLANES - taps_back + ts, :] * cw_ref[0:1, :]
    for tap in range(1, LRU_CONV_W):
        off = SUBLANES - taps_back + tap
        xc = xc + xbuf_ref[off:off + ts, :] * cw_ref[tap:tap + 1, :]
    xc = xc + cb_ref[...]
    xbuf_ref[0:SUBLANES, :] = xbuf_ref[ts:ts + SUBLANES, :]
    xcb = xc.astype(BF16)
    r_pre = [_dot(xcb[:, d * half_w:(d + 1) * half_w], wa_ref[d]) for d in range(2)]
    i_pre = [_dot(xcb[:, d * half_w:(d + 1) * half_w], wx_ref[d]) for d in range(2)]
    neg_lam = -lam_ref[...]
    softplus = jnp.maximum(neg_lam, 0.0) + jnp.log1p(jnp.exp(-jnp.abs(neg_lam)))

    def lru_inputs(r0, rows):
        rs = slice(r0, r0 + rows)
        r_gate = _sigmoid(jnp.concatenate([r_pre[0][rs], r_pre[1][rs]], axis=1) + ba_ref[...])
        i_gate = _sigmoid(jnp.concatenate([i_pre[0][rs], i_pre[1][rs]], axis=1) + bx_ref[...])
        log_a = -LRU_C * r_gate * softplus
        th = jnp.tanh(log_a)
        return jnp.exp(log_a), jnp.sqrt(-2.0 * th / (1.0 - th)) * (i_gate * xc[rs])

    f_low = z_ref[:, Z_F:Z_F + LANES].astype(BF16)
    gate_in = _dot(f_low, wf_ref[...]) + bf_ref[...]
    log_f = (jnp.minimum(gate_in, 0.0) - jnp.log1p(jnp.exp(-jnp.abs(gate_in)))) * (1.0 / GLA_GATE_TAU)
    row = lax.broadcasted_iota(jnp.int32, (ts, ts), 0)
    col = lax.broadcasted_iota(jnp.int32, (ts, ts), 1)
    in_chunk_causal = (row // c_len == col // c_len) & (col <= row)
    tri = jnp.where(in_chunk_causal, 1.0, 0.0).astype(BF16)
    hi, mid, lo = _split3(log_f)
    b_all = _dot(tri, hi) + _dot(tri, mid) + _dot(tri, lo)
    b_last = [b_all[(c + 1) * c_len - 1:(c + 1) * c_len] for c in range(n_chunks)]
    b_last_rows = jnp.concatenate([jnp.broadcast_to(bl, (c_len, GLA_KW)) for bl in b_last], axis=0)
    decay = [jnp.exp(bl) for bl in b_last]
    q_all = z_ref[:, Z_Q:Z_Q + GLA_KW]
    k_all = z_ref[:, Z_K:Z_K + GLA_KW]
    q_dec = (q_all * (GLA_DK ** -0.5) * jnp.exp(b_all)).astype(BF16)
    k_dec = (k_all * jnp.exp(-b_all)).astype(BF16)
    k_end = (k_all * jnp.exp(b_last_rows - b_all)).astype(BF16)
    lane = lax.broadcasted_iota(jnp.int32, (1, LANES), 1)
    head_lanes = (jnp.where(lane < GLA_DK, 1.0, 0.0).astype(BF16), jnp.where(lane < GLA_DK, 0.0, 1.0).astype(BF16))
    row_chunk = lax.broadcasted_iota(jnp.int32, (ts, LANES), 0) // c_len
    zero_k = jnp.zeros((ts, LANES), BF16)

    a_parts, u_parts = [], []
    for pair in range(GLA_HEADS // 2):
        kl = slice(pair * LANES, (pair + 1) * LANES)
        q_pair = q_dec[:, kl]
        q_heads = jnp.concatenate([q_pair * head_lanes[0], q_pair * head_lanes[1]], axis=0)
        scores = _dot_nt(q_heads, k_dec[:, kl])
        k_end_pair = k_end[:, kl]
        k_end_blocks = jnp.concatenate(
            [jnp.where(row_chunk == c, k_end_pair, zero_k) for c in range(n_chunks)], axis=1)
        for half in range(2):
            hd = 2 * pair + half
            vl = slice(hd * GLA_DV, (hd + 1) * GLA_DV)
            q_h = q_heads[half * ts:(half + 1) * ts]
            att = jnp.where(in_chunk_causal, scores[half * ts:(half + 1) * ts], 0.0).astype(BF16)
            v_h = z_ref[:, Z_V + hd * GLA_DV:Z_V + (hd + 1) * GLA_DV].astype(BF16)
            o = _dot(att, v_h)
            kv_t = _dot_tn(v_h, k_end_blocks)
            s_t = st_ref[hd]
            o_inter = []
            for c in range(n_chunks):
                o_inter.append(_dot_nt(q_h[c * c_len:(c + 1) * c_len], s_t.astype(BF16)))
                s_t = s_t * decay[c][:, kl] + kv_t[:, c * LANES:(c + 1) * LANES]
            st_ref[hd] = s_t
            o = o + jnp.concatenate(o_inter, axis=0)
            o = o * lax.rsqrt(jnp.mean(o * o, axis=-1, keepdims=True) + EPS)
            r_h = z_ref[:, Z_R + hd * GLA_DV:Z_R + (hd + 1) * GLA_DV]
            o_ref[:, vl] = (o * gn_ref[:, vl] * (r_h * _sigmoid(r_h))).astype(o_ref.dtype)
            a_blk, u_blk = lru_inputs(hd * (ts // GLA_HEADS), ts // GLA_HEADS)
            a_parts.append(a_blk)
            u_parts.append(u_blk)

    a_all = jnp.concatenate(a_parts, axis=0)
    u_all = jnp.concatenate(u_parts, axis=0)
    srow = lax.broadcasted_iota(jnp.int32, (SUBLANES, LRU_WIDTH), 0)
    local = []
    for g in range(ts // SUBLANES):
        a = a_all[g * SUBLANES:(g + 1) * SUBLANES]
        u = u_all[g * SUBLANES:(g + 1) * SUBLANES]
        for s in (1, 2, 4):
            keep = srow >= s
            a_sh = jnp.where(keep, pltpu.roll(a, s, 0), 1.0)
            u_sh = jnp.where(keep, pltpu.roll(u, s, 0), 0.0)
            u = a * u_sh + u
            a = a * a_sh
        local.append((a, u))
    carry = hc_ref[...]
    h_groups = []
    for a, u in local:
        h_groups.append(a * carry + u)
        carry = a[SUBLANES - 1:SUBLANES, :] * carry + u[SUBLANES - 1:SUBLANES, :]
    hc_ref[...] = carry
    g_in = z_ref[:, Z_G:Z_G + LRU_WIDTH]
    gelu = 0.5 * g_in * (1.0 + jnp.tanh(math.sqrt(2.0 / math.pi) * (g_in + 0.044715 * (g_in * g_in * g_in))))
    o_ref[:, GLA_VW:GLA_VW + LRU_WIDTH] = (jnp.concatenate(h_groups, axis=0) * gelu).astype(o_ref.dtype)


def _hybrid_core(z, wf, bf, gn, cw, cb, wa, ba, wx, bx, lam):
    b, s, _ = z.shape
    const = lambda shape: pl.BlockSpec(shape, lambda i, j: (0,) * len(shape))
    return pl.pallas_call(
        _hybrid_body,
        grid=(b, s // HYB_TS),
        in_specs=[
            pl.BlockSpec((None, HYB_TS, HYB_ZW), lambda i, j: (i, j, 0)),
            const((LANES, GLA_KW)), const((1, GLA_KW)), const((1, GLA_VW)),
            const((LRU_CONV_W, LRU_WIDTH)), const((1, LRU_WIDTH)),
            const((2, LRU_WIDTH // 2, LRU_WIDTH // 2)), const((1, LRU_WIDTH)),
            const((2, LRU_WIDTH // 2, LRU_WIDTH // 2)), const((1, LRU_WIDTH)),
            const((1, LRU_WIDTH)),
        ],
        out_specs=pl.BlockSpec((None, HYB_TS, GLA_VW + LRU_WIDTH), lambda i, j: (i, j, 0)),
        out_shape=jax.ShapeDtypeStruct((b, s, GLA_VW + LRU_WIDTH), BF16),
        scratch_shapes=[
            pltpu.VMEM((GLA_HEADS, GLA_DV, LANES), F32),
            pltpu.VMEM((HYB_TS + 2 * SUBLANES, LRU_WIDTH), F32),
            pltpu.VMEM((1, LRU_WIDTH), F32),
        ],
        compiler_params=_params("parallel", "arbitrary"),
        name="hybrid_core",
    )(z, wf, bf, gn, cw, cb, wa, ba, wx, bx, lam)


def _block_diag_halves(w):
    g, bw, _ = w.shape
    eye = jnp.eye(g // 2, dtype=w.dtype)
    w = w.reshape(2, g // 2, bw, bw)
    return (eye[None, :, None, :, None] * w[:, :, :, None, :]).reshape(2, g * bw // 2, g * bw // 2)


def _hybrid_in_weights(w_in):
    q_w, k_w, v_w, r_w, f_w, xr_w, gr_w = jnp.split(
        w_in, np.cumsum([GLA_KW, GLA_KW, GLA_VW, GLA_VW, GLA_GATE_RANK, LRU_WIDTH]).tolist(), axis=1)
    f_w = jnp.pad(f_w, ((0, 0), (0, LANES - GLA_GATE_RANK)))
    return jnp.concatenate([q_w, k_w, v_w, r_w, xr_w, gr_w, f_w], axis=1).astype(BF16)


def _hybrid_mix(z, w_fup, b_f, gla_norm, conv_w, conv_b, w_a, b_a, w_x, b_x, lam, batch):
    t = z.shape[0]
    wf = jnp.pad(w_fup, ((0, LANES - GLA_GATE_RANK), (0, 0))).astype(BF16)
    mix = _hybrid_core(
        z.reshape(batch, t // batch, HYB_ZW), wf, b_f.reshape(1, GLA_KW), gla_norm.reshape(1, GLA_VW),
        conv_w, conv_b.reshape(1, LRU_WIDTH), _block_diag_halves(w_a).astype(BF16), b_a.reshape(1, LRU_WIDTH),
        _block_diag_halves(w_x).astype(BF16), b_x.reshape(1, LRU_WIDTH), lam.reshape(1, LRU_WIDTH))
    return mix.reshape(t, GLA_VW + LRU_WIDTH)


def _t5_bucket_table():
    max_exact = REL_BUCKETS // 2
    dist = np.arange(SWA_BLOCK)
    d = np.maximum(dist, 1).astype(np.float32)
    large = max_exact + (np.log(d / max_exact) / math.log(REL_MAX_DIST / max_exact)
                         * (REL_BUCKETS - max_exact)).astype(np.int32)
    bucket = np.where(dist < max_exact, dist, np.minimum(large, REL_BUCKETS - 1)).astype(np.int32)
    i = np.arange(SWA_BLOCK)[:, None]
    j = np.arange(SWA_BLOCK)[None, :]
    return bucket[(i - j) % SWA_BLOCK]


def _roll_half_lanes(x):
    words = pltpu.bitcast(x, jnp.int32)
    return pltpu.bitcast(pltpu.roll(words, LANES // 2, 1), BF16)


def _swa_body(rel_ref, sink_ref, bkt_ref, q_ref, kvc_ref, kvp_ref, o_ref, bias_ref, kd_ref, vo_ref):
    blk = SWA_BLOCK
    dh = SWA_HEAD_DIM
    tq = SWA_TQ
    first_tile = pl.program_id(1) == 0
    row = lax.broadcasted_iota(jnp.int32, (blk, blk), 0)
    col = lax.broadcasted_iota(jnp.int32, (blk, blk), 1)
    in_cur = col <= row
    lo_half = col < dh

    @pl.when((pl.program_id(0) == 0) & first_tile)
    def _():
        bkt = bkt_ref[...]
        for h in range(SWA_HEADS):
            acc = jnp.zeros((blk, blk), F32)
            for k in range(REL_BUCKETS):
                acc = jnp.where(bkt == k, rel_ref[k, h], acc)
            bias_ref[0, h] = acc
            bias_ref[1, h] = jnp.where(in_cur, acc, MASK_VALUE)
        band_lane = lax.broadcasted_iota(jnp.int32, (tq + blk, LANES), 1)
        ones_lo = jnp.where(band_lane < dh, 1.0, 0.0).astype(BF16)
        for kv in range(SWA_KV_HEADS):
            vo_ref[kv, 0, :, LANES:] = ones_lo
            vo_ref[kv, 1, :, LANES:] = 1.0 - ones_lo

    def build_band(src_ref, r0, rows):
        lo = lax.broadcasted_iota(jnp.int32, (rows, LANES), 1) < dh
        zero = jnp.zeros((rows, LANES), BF16)
        for kv in range(SWA_KV_HEADS):
            c0 = LANES * (kv // 2)
            k_pair = src_ref[:, c0:c0 + LANES]
            v_pair = src_ref[:, SWA_KV_W + c0:SWA_KV_W + c0 + LANES]
            k_rot = _roll_half_lanes(k_pair)
            v_rot = _roll_half_lanes(v_pair)
            k_own, k_other, v_own, v_other = ((k_pair, k_rot, v_pair, v_rot) if kv % 2 == 0
                                              else (k_rot, k_pair, v_rot, v_pair))
            kd_ref[kv, r0:r0 + rows, :] = jnp.where(lo, k_own, k_other)
            vo_ref[kv, 0, r0:r0 + rows, :LANES] = jnp.where(lo, v_own, zero)
            vo_ref[kv, 1, r0:r0 + rows, :LANES] = jnp.where(lo, zero, v_other)

    build_band(kvp_ref, 0, blk)
    build_band(kvc_ref, blk, tq)

    lane = lax.broadcasted_iota(jnp.int32, (1, LANES), 1)
    q_scale = (jnp.where(lane < dh, dh ** -0.5, 0.0).astype(BF16), jnp.where(lane < dh, 0.0, dh ** -0.5).astype(BF16))
    zero_p = jnp.zeros((blk, blk), BF16)

    def sub_block(n, carry):
        r0 = pl.multiple_of(n * blk, blk)
        first = jnp.where(first_tile & (n == 0), 1, 0)
        scores = []
        for kv in range(SWA_KV_HEADS):
            k_band = kd_ref[kv, pl.ds(r0, 2 * blk), :]
            for pr in range(SWA_GROUP // 2):
                q_pair = q_ref[pl.ds(r0, blk), pl.ds(LANES * (2 * kv + pr), LANES)]
                for half in range(2):
                    h = SWA_GROUP * kv + 2 * pr + half
                    s = _dot_nt(q_pair * q_scale[half], k_band)
                    scores.append(jnp.where(in_cur, s[:, blk:], s[:, :blk]) + bias_ref[first, h])
        maxes = [jnp.maximum(jnp.max(s, axis=-1, keepdims=True), sink_ref[h]) for h, s in enumerate(scores)]
        exps = [jnp.exp(s - m).astype(BF16) for s, m in zip(scores, maxes)]
        for kv in range(SWA_KV_HEADS):
            for pr in range(SWA_GROUP // 2):
                h0 = SWA_GROUP * kv + 2 * pr
                acc = None
                for half in range(2):
                    e = exps[h0 + half]
                    p_band = jnp.concatenate([jnp.where(in_cur, zero_p, e), jnp.where(in_cur, e, zero_p)], axis=1)
                    part = _dot(p_band, vo_ref[kv, half, pl.ds(r0, 2 * blk), :])
                    acc = part if acc is None else acc + part
                sink_term = jnp.where(lo_half, jnp.exp(sink_ref[h0] - maxes[h0]),
                                      jnp.exp(sink_ref[h0 + 1] - maxes[h0 + 1]))
                out = acc[:, :LANES] / (acc[:, LANES:] + sink_term)
                o_ref[pl.ds(r0, blk), pl.ds(LANES * (2 * kv + pr), LANES)] = out.astype(o_ref.dtype)
        return carry

    lax.fori_loop(0, tq // blk, sub_block, 0)


def _swa_core(z, rel_bias, sinks):
    b, s, _ = z.shape
    kv_col = SWA_Q_W // (2 * SWA_KV_W)
    blocks_per_tile = SWA_TQ // SWA_BLOCK
    smem = lambda: pl.BlockSpec(memory_space=pltpu.SMEM)
    return pl.pallas_call(
        _swa_body,
        grid=(b, s // SWA_TQ),
        in_specs=[
            smem(), smem(),
            pl.BlockSpec((SWA_BLOCK, SWA_BLOCK), lambda i, j: (0, 0)),
            pl.BlockSpec((None, SWA_TQ, SWA_Q_W), lambda i, j: (i, j, 0)),
            pl.BlockSpec((None, SWA_TQ, 2 * SWA_KV_W), lambda i, j: (i, j, kv_col)),
            pl.BlockSpec((None, SWA_BLOCK, 2 * SWA_KV_W),
                         lambda i, j: (i, jnp.maximum(j * blocks_per_tile - 1, 0), kv_col)),
        ],
        out_specs=pl.BlockSpec((None, SWA_TQ, SWA_Q_W), lambda i, j: (i, j, 0)),
        out_shape=jax.ShapeDtypeStruct((b, s, SWA_Q_W), BF16),
        scratch_shapes=[
            pltpu.VMEM((2, SWA_HEADS, SWA_BLOCK, SWA_BLOCK), F32),
            pltpu.VMEM((SWA_KV_HEADS, SWA_TQ + SWA_BLOCK, LANES), BF16),
            pltpu.VMEM((SWA_KV_HEADS, 2, SWA_TQ + SWA_BLOCK, 2 * LANES), BF16),
        ],
        compiler_params=_params("arbitrary", "arbitrary"),
        name="swa_core",
    )(rel_bias, sinks, jnp.asarray(_t5_bucket_table()), z, z, z)


def _swa_mix(z, rel_bias, sinks, batch):
    t = z.shape[0]
    return _swa_core(z.reshape(batch, t // batch, SWA_Q_W + 2 * SWA_KV_W), rel_bias, sinks).reshape(t, SWA_Q_W)


def kernel(x, p, rel_bias, final_norm, ffn1_norm, ffn1_w_gate, ffn1_w_up, ffn1_w_down, mix_norm, ffn2_norm,
           ffn2_w_gate, ffn2_w_up, ffn2_w_down, ple_norm, ple_w_proj, ple_w_gate, hyb_w_in, hyb_w_out, gla_w_fup,
           gla_b_f, gla_norm, lru_conv_w, lru_conv_b, lru_w_a, lru_b_a, lru_w_x, lru_b_x, lru_lambda, swa_w_qkv,
           swa_b_qkv, swa_w_o, swa_b_o, swa_sinks):
    batch, seq, _ = x.shape
    t = batch * seq
    x = x.reshape(t, D_MODEL)
    p = p.reshape(DEPTH, t, PLE_DIM)
    gains = lambda g: g.reshape(DEPTH, 1, D_MODEL)
    ffn1 = (gains(ffn1_norm), ffn1_w_gate.astype(BF16), ffn1_w_up.astype(BF16), ffn1_w_down.astype(BF16))
    ffn2 = (gains(ffn2_norm), ffn2_w_gate.astype(BF16), ffn2_w_up.astype(BF16), ffn2_w_down.astype(BF16))
    ple = (gains(ple_norm), ple_w_gate.astype(BF16), ple_w_proj.astype(BF16))
    mix_g = gains(mix_norm)
    final_g = final_norm.reshape(1, D_MODEL)
    no_bias_in = jnp.zeros((1, HYB_ZW), F32)
    no_bias_out = jnp.zeros((1, D_MODEL), F32)
    for i in range(DEPTH):
        if i % 2 == 0:
            e = i // 2
            x, z = _pre_stage(x, i, ffn1, mix_g, _hybrid_in_weights(hyb_w_in[e]), no_bias_in, F32, PRE_HYB_TM,
                              "pre_hyb")
            mix = _hybrid_mix(z, gla_w_fup[e], gla_b_f[e], gla_norm[e], lru_conv_w[e], lru_conv_b[e], lru_w_a[e],
                              lru_b_a[e], lru_w_x[e], lru_b_x[e], lru_lambda[e], batch)
            x = _post_stage(mix, x, p, i, hyb_w_out[e].astype(BF16), no_bias_out, ffn2, ple, final_g, "post_hyb")
        else:
            o = i // 2
            x, z = _pre_stage(x, i, ffn1, mix_g, swa_w_qkv[o].astype(BF16), swa_b_qkv[o].reshape(1, -1), BF16,
                              TOK_TM, "pre_swa")
            mix = _swa_mix(z, rel_bias, swa_sinks[o], batch)
            x = _post_stage(mix, x, p, i, swa_w_o[o].astype(BF16), swa_b_o[o].reshape(1, D_MODEL), ffn2, ple,
                            final_g, "post_swa")
    return x.reshape(batch, seq, D_MODEL)
```

```python
import functools
import math

import jax
import jax.numpy as jnp
import numpy as np
from jax import lax
from jax.experimental import pallas as pl
from jax.experimental.pallas import tpu as pltpu

F32 = jnp.float32
BF16 = jnp.bfloat16

D_MODEL = 1024
DEPTH = 4
PLE_DIM = 256
D_FF = 2816
EPS = 1e-6

GLA_HEADS = 4
GLA_DK = 64
GLA_DV = 128
GLA_KW = GLA_HEADS * GLA_DK
GLA_VW = GLA_HEADS * GLA_DV
GLA_GATE_RANK = 16
GLA_GATE_TAU = 16.0
GLA_CHUNK = 64
LRU_WIDTH = 512
LRU_BLOCKS = 8
LRU_CONV_W = 4
LRU_C = 8.0

SWA_HEADS = 16
SWA_KV_HEADS = 4
SWA_HEAD_DIM = 64
SWA_GROUP = SWA_HEADS // SWA_KV_HEADS
SWA_BLOCK = 128
SWA_Q_W = SWA_HEADS * SWA_HEAD_DIM
SWA_KV_W = SWA_KV_HEADS * SWA_HEAD_DIM
REL_BUCKETS = 32
REL_MAX_DIST = 128
MASK_VALUE = -1e30

LANES = 128
SUBLANES = 8
BF16_SUBLANES = 16
VMEM_LIMIT_BYTES = 56 * 1024 * 1024

TOK_TM = 512
PRE_HYB_TM = 512
FFN_TF = 256
HYB_TS = 256
SWA_TQ = 512

Z_Q = 0
Z_K = Z_Q + GLA_KW
Z_V = Z_K + GLA_KW
Z_R = Z_V + GLA_VW
Z_X = Z_R + GLA_VW
Z_G = Z_X + LRU_WIDTH
Z_F = Z_G + LRU_WIDTH
HYB_ZW = Z_F + LANES


def _params(*semantics):
    return pltpu.CompilerParams(dimension_semantics=semantics, vmem_limit_bytes=VMEM_LIMIT_BYTES)


def _rms(x, g):
    return x * lax.rsqrt(jnp.mean(x * x, axis=-1, keepdims=True) + EPS) * g


def _sigmoid(x):
    return 1.0 / (1.0 + jnp.exp(-x))


def _dot(a, b):
    return jnp.dot(a, b, preferred_element_type=F32)


def _dot_nt(a, b):
    return lax.dot_general(a, b, (((1,), (1,)), ((), ())), preferred_element_type=F32)


def _dot_tn(a, b):
    return lax.dot_general(a, b, (((0,), (0,)), ((), ())), preferred_element_type=F32)


def _ffn_apply(x, g_ref, wg_ref, wu_ref, wd_ref):
    xn = _rms(x, g_ref[...]).astype(BF16)
    acc = x
    starts = range(0, D_FF, FFN_TF)
    gate_up = lambda c0: (_dot(xn, wg_ref[:, c0:c0 + FFN_TF]), _dot(xn, wu_ref[:, c0:c0 + FFN_TF]))
    nxt = gate_up(starts[0])
    for idx, c0 in enumerate(starts):
        a, b = nxt
        if idx + 1 < len(starts):
            nxt = gate_up(starts[idx + 1])
        h = (0.5 * a) * _sigmoid(a) * b
        acc = acc + _dot(h.astype(BF16), wd_ref[c0:c0 + FFN_TF, :])
    return acc


def _cast_next_weights(refs):
    half = len(refs) // 2
    for src_ref, dst_ref in zip(refs[:half], refs[half:]):
        dst_ref[...] = src_ref[...].astype(BF16)


def _pre_body(x_ref, g1_ref, wg_ref, wu_ref, wd_ref, gm_ref, win_ref, bin_ref, *rest):
    nxt_in, (xo_ref, z_ref), nxt_out = rest[:-5], rest[-5:-3], rest[-3:]
    _cast_next_weights(nxt_in + nxt_out)
    x = _ffn_apply(x_ref[...], g1_ref, wg_ref, wu_ref, wd_ref)
    xo_ref[...] = x
    z_ref[...] = (_dot(_rms(x, gm_ref[...]).astype(BF16), win_ref[...]) + bin_ref[...]).astype(z_ref.dtype)


def _post_body(mix_ref, x_ref, p_ref, wo_ref, bo_ref, g2_ref, wg_ref, wu_ref, wd_ref, gp_ref, wpg_ref, wpp_ref,
               fg_ref, *rest, final):
    n_next = (len(rest) - 1) // 2
    nxt_in, o_ref, nxt_out = rest[:n_next], rest[n_next], rest[n_next + 1:]
    _cast_next_weights(nxt_in + nxt_out)
    x = x_ref[...] + (_dot(mix_ref[...], wo_ref[...]) + bo_ref[...])
    x = _ffn_apply(x, g2_ref, wg_ref, wu_ref, wd_ref)
    gate = _sigmoid(_dot(_rms(x, gp_ref[...]).astype(BF16), wpg_ref[...]))
    x = x + gate * _dot(p_ref[...].astype(BF16), wpp_ref[...])
    if final:
        x = _rms(x, fg_ref[...])
    o_ref[...] = x


def _rows(width, tm=TOK_TM):
    return pl.BlockSpec((tm, width), lambda i: (i, 0))


def _resident(array, layer=None):
    if layer is None:
        return pl.BlockSpec(array.shape, lambda i: (0, 0), pipeline_mode=pl.Buffered(1))
    return pl.BlockSpec((None,) + array.shape[1:], lambda i: (layer, 0, 0), pipeline_mode=pl.Buffered(1))


def _cast_slices(weights, layer, steps):
    in_specs, out_specs, out_shapes = [], [], []
    for w in weights:
        _, rows, cols = w.shape
        span = 1 if (rows // steps) % BF16_SUBLANES == 0 else 2
        block = rows * span // steps
        assert rows % block == 0 and block % BF16_SUBLANES == 0
        in_specs.append(pl.BlockSpec((None, block, cols), lambda i, span=span: (layer, i // span, 0)))
        out_specs.append(pl.BlockSpec((block, cols), lambda i, span=span: (i // span, 0)))
        out_shapes.append(jax.ShapeDtypeStruct((rows, cols), BF16))
    return in_specs, out_specs, out_shapes


def _ffn_specs(ffn, layer):
    return [_resident(ffn[0], layer)] + [_resident(w) for w in ffn[1:]]


def _pre_stage(x, layer, ffn, mix_g, w_in, b_in, z_dtype, tm, next_ffn, name):
    t = x.shape[0]
    n = w_in.shape[1]
    nxt_w, nxt_layer = next_ffn
    cast_in, cast_out, cast_shapes = _cast_slices(nxt_w, nxt_layer, t // tm)
    x, z, *nxt = pl.pallas_call(
        _pre_body,
        grid=(t // tm,),
        in_specs=[_rows(D_MODEL, tm)] + _ffn_specs(ffn, layer)
        + [_resident(mix_g, layer), _resident(w_in), _resident(b_in)] + cast_in,
        out_specs=[_rows(D_MODEL, tm), _rows(n, tm)] + cast_out,
        out_shape=[jax.ShapeDtypeStruct((t, D_MODEL), F32), jax.ShapeDtypeStruct((t, n), z_dtype)] + cast_shapes,
        compiler_params=_params("arbitrary"),
        name=name,
    )(x, *ffn, mix_g, w_in, b_in, *nxt_w)
    return x, z, nxt


def _post_stage(mix, x, p, layer, w_o, b_o, ffn, ple, final_g, next_ffn, name):
    t = x.shape[0]
    nxt_w, nxt_layer = next_ffn
    cast_in, cast_out, cast_shapes = _cast_slices(nxt_w, nxt_layer, t // TOK_TM)
    x, *nxt = pl.pallas_call(
        functools.partial(_post_body, final=(layer == DEPTH - 1)),
        grid=(t // TOK_TM,),
        in_specs=[_rows(mix.shape[1]), _rows(D_MODEL),
                  pl.BlockSpec((None, TOK_TM, PLE_DIM), lambda i: (layer, i, 0)),
                  _resident(w_o), _resident(b_o)]
        + _ffn_specs(ffn, layer) + [_resident(w, layer) for w in ple] + [_resident(final_g)] + cast_in,
        out_specs=[_rows(D_MODEL)] + cast_out,
        out_shape=[jax.ShapeDtypeStruct((t, D_MODEL), F32)] + cast_shapes,
        compiler_params=_params("arbitrary"),
        name=name,
    )(mix, x, p, w_o, b_o, *ffn, *ple, final_g, *nxt_w)
    return x, nxt


def _split3(x):
    hi = x.astype(BF16)
    r1 = x - hi.astype(F32)
    mid = r1.astype(BF16)
    lo = (r1 - mid.astype(F32)).astype(BF16)
    return hi, mid, lo


def _hybrid_body(z_ref, wf_ref, bf_ref, gn_ref, cw_ref, cb_ref, wa_ref, ba_ref, wx_ref, bx_ref, lam_ref,
                 o_ref, st_ref, xbuf_ref, hc_ref):
    ts = HYB_TS
    c_len = GLA_CHUNK
    n_chunks = ts // c_len
    half_w = LRU_WIDTH // 2

    @pl.when(pl.program_id(1) == 0)
    def _():
        st_ref[...] = jnp.zeros_like(st_ref)
        xbuf_ref[0:SUBLANES, :] = jnp.zeros((SUBLANES, LRU_WIDTH), F32)
        hc_ref[...] = jnp.zeros_like(hc_ref)

    xbuf_ref[SUBLANES:SUBLANES + ts, :] = z_ref[:, Z_X:Z_X + LRU_WIDTH]
    taps_back = LRU_CONV_W - 1
    xc = xbuf_ref[SUBLANES - taps_back:SUBLANES - taps_back + ts, :] * cw_ref[0:1, :]
    for tap in range(1, LRU_CONV_W):
        off = SUBLANES - taps_back + tap
        xc = xc + xbuf_ref[off:off + ts, :] * cw_ref[tap:tap + 1, :]
    xc = xc + cb_ref[...]
    xbuf_ref[0:SUBLANES, :] = xbuf_ref[ts:ts + SUBLANES, :]
    xcb = xc.astype(BF16)
    r_pre = [_dot(xcb[:, d * half_w:(d + 1) * half_w], wa_ref[d]) for d in range(2)]
    i_pre = [_dot(xcb[:, d * half_w:(d + 1) * half_w], wx_ref[d]) for d in range(2)]
    neg_lam = -lam_ref[...]
    softplus = jnp.maximum(neg_lam, 0.0) + jnp.log1p(jnp.exp(-jnp.abs(neg_lam)))

    def lru_inputs(r0, rows):
        rs = slice(r0, r0 + rows)
        r_gate = _sigmoid(jnp.concatenate([r_pre[0][rs], r_pre[1][rs]], axis=1) + ba_ref[...])
        i_gate = _sigmoid(jnp.concatenate([i_pre[0][rs], i_pre[1][rs]], axis=1) + bx_ref[...])
        log_a = -LRU_C * r_gate * softplus
        th = jnp.tanh(log_a)
        return jnp.exp(log_a), jnp.sqrt(-2.0 * th / (1.0 - th)) * (i_gate * xc[rs])

    f_low = z_ref[:, Z_F:Z_F + LANES].astype(BF16)
    gate_in = _dot(f_low, wf_ref[...]) + bf_ref[...]
    log_f = (jnp.minimum(gate_in, 0.0) - jnp.log1p(jnp.exp(-jnp.abs(gate_in)))) * (1.0 / GLA_GATE_TAU)
    row = lax.broadcasted_iota(jnp.int32, (ts, ts), 0)
    col = lax.broadcasted_iota(jnp.int32, (ts, ts), 1)
    in_chunk_causal = (row // c_len == col // c_len) & (col <= row)
    tri = jnp.where(in_chunk_causal, 1.0, 0.0).astype(BF16)
    hi, mid, lo = _split3(log_f)
    b_all = _dot(tri, hi) + _dot(tri, mid) + _dot(tri, lo)
    b_last = [b_all[(c + 1) * c_len - 1:(c + 1) * c_len] for c in range(n_chunks)]
    b_last_rows = jnp.concatenate([jnp.broadcast_to(bl, (c_len, GLA_KW)) for bl in b_last], axis=0)
    decay = [jnp.exp(bl) for bl in b_last]
    q_all = z_ref[:, Z_Q:Z_Q + GLA_KW]
    k_all = z_ref[:, Z_K:Z_K + GLA_KW]
    q_dec = (q_all * (GLA_DK ** -0.5) * jnp.exp(b_all)).astype(BF16)
    k_dec = (k_all * jnp.exp(-b_all)).astype(BF16)
    k_end = (k_all * jnp.exp(b_last_rows - b_all)).astype(BF16)
    lane = lax.broadcasted_iota(jnp.int32, (1, LANES), 1)
    head_lanes = (jnp.where(lane < GLA_DK, 1.0, 0.0).astype(BF16), jnp.where(lane < GLA_DK, 0.0, 1.0).astype(BF16))
    row_chunk = lax.broadcasted_iota(jnp.int32, (ts, LANES), 0) // c_len
    zero_k = jnp.zeros((ts, LANES), BF16)

    a_parts, u_parts = [], []
    for pair in range(GLA_HEADS // 2):
        kl = slice(pair * LANES, (pair + 1) * LANES)
        q_pair = q_dec[:, kl]
        q_heads = jnp.concatenate([q_pair * head_lanes[0], q_pair * head_lanes[1]], axis=0)
        scores = _dot_nt(q_heads, k_dec[:, kl])
        k_end_pair = k_end[:, kl]
        k_end_blocks = jnp.concatenate(
            [jnp.where(row_chunk == c, k_end_pair, zero_k) for c in range(n_chunks)], axis=1)
        for half in range(2):
            hd = 2 * pair + half
            vl = slice(hd * GLA_DV, (hd + 1) * GLA_DV)
            q_h = q_heads[half * ts:(half + 1) * ts]
            att = jnp.where(in_chunk_causal, scores[half * ts:(half + 1) * ts], 0.0).astype(BF16)
            v_h = z_ref[:, Z_V + hd * GLA_DV:Z_V + (hd + 1) * GLA_DV].astype(BF16)
            o = _dot(att, v_h)
            kv_t = _dot_tn(v_h, k_end_blocks)
            s_t = st_ref[hd]
            o_inter = []
            for c in range(n_chunks):
                o_inter.append(_dot_nt(q_h[c * c_len:(c + 1) * c_len], s_t.astype(BF16)))
                s_t = s_t * decay[c][:, kl] + kv_t[:, c * LANES:(c + 1) * LANES]
            st_ref[hd] = s_t
            o = o + jnp.concatenate(o_inter, axis=0)
            o = o * lax.rsqrt(jnp.mean(o * o, axis=-1, keepdims=True) + EPS)
            r_h = z_ref[:, Z_R + hd * GLA_DV:Z_R + (hd + 1) * GLA_DV]
            o_ref[:, vl] = (o * gn_ref[:, vl] * (r_h * _sigmoid(r_h))).astype(o_ref.dtype)
            a_blk, u_blk = lru_inputs(hd * (ts // GLA_HEADS), ts // GLA_HEADS)
            a_parts.append(a_blk)
            u_parts.append(u_blk)

    a_all = jnp.concatenate(a_parts, axis=0)
    u_all = jnp.concatenate(u_parts, axis=0)
    srow = lax.broadcasted_iota(jnp.int32, (SUBLANES, LRU_WIDTH), 0)
    local = []
    for g in range(ts // SUBLANES):
        a = a_all[g * SUBLANES:(g + 1) * SUBLANES]
        u = u_all[g * SUBLANES:(g + 1) * SUBLANES]
        for s in (1, 2, 4):
            keep = srow >= s
            a_sh = jnp.where(keep, pltpu.roll(a, s, 0), 1.0)
            u_sh = jnp.where(keep, pltpu.roll(u, s, 0), 0.0)
            u = a * u_sh + u
            a = a * a_sh
        local.append((a, u))
    carry = hc_ref[...]
    h_groups = []
    for a, u in local:
        h_groups.append(a * carry + u)
        carry = a[SUBLANES - 1:SUBLANES, :] * carry + u[SUBLANES - 1:SUBLANES, :]
    hc_ref[...] = carry
    g_in = z_ref[:, Z_G:Z_G + LRU_WIDTH]
    gelu = 0.5 * g_in * (1.0 + jnp.tanh(math.sqrt(2.0 / math.pi) * (g_in + 0.044715 * (g_in * g_in * g_in))))
    o_ref[:, GLA_VW:GLA_VW + LRU_WIDTH] = (jnp.concatenate(h_groups, axis=0) * gelu).astype(o_ref.dtype)


def _hybrid_core(z, wf, bf, gn, cw, cb, wa, ba, wx, bx, lam):
    b, s, _ = z.shape
    const = lambda shape: pl.BlockSpec(shape, lambda i, j: (0,) * len(shape))
    return pl.pallas_call(
        _hybrid_body,
        grid=(b, s // HYB_TS),
        in_specs=[
            pl.BlockSpec((None, HYB_TS, HYB_ZW), lambda i, j: (i, j, 0)),
            const((LANES, GLA_KW)), const((1, GLA_KW)), const((1, GLA_VW)),
            const((LRU_CONV_W, LRU_WIDTH)), const((1, LRU_WIDTH)),
            const((2, LRU_WIDTH // 2, LRU_WIDTH // 2)), const((1, LRU_WIDTH)),
            const((2, LRU_WIDTH // 2, LRU_WIDTH // 2)), const((1, LRU_WIDTH)),
            const((1, LRU_WIDTH)),
        ],
        out_specs=pl.BlockSpec((None, HYB_TS, GLA_VW + LRU_WIDTH), lambda i, j: (i, j, 0)),
        out_shape=jax.ShapeDtypeStruct((b, s, GLA_VW + LRU_WIDTH), BF16),
        scratch_shapes=[
            pltpu.VMEM((GLA_HEADS, GLA_DV, LANES), F32),
            pltpu.VMEM((HYB_TS + 2 * SUBLANES, LRU_WIDTH), F32),
            pltpu.VMEM((1, LRU_WIDTH), F32),
        ],
        compiler_params=_params("parallel", "arbitrary"),
        name="hybrid_core",
    )(z, wf, bf, gn, cw, cb, wa, ba, wx, bx, lam)


def _block_diag_halves(w):
    g, bw, _ = w.shape
    eye = jnp.eye(g // 2, dtype=w.dtype)
    w = w.reshape(2, g // 2, bw, bw)
    return (eye[None, :, None, :, None] * w[:, :, :, None, :]).reshape(2, g * bw // 2, g * bw // 2)


def _hybrid_in_weights(w_in):
    q_w, k_w, v_w, r_w, f_w, xr_w, gr_w = jnp.split(
        w_in, np.cumsum([GLA_KW, GLA_KW, GLA_VW, GLA_VW, GLA_GATE_RANK, LRU_WIDTH]).tolist(), axis=1)
    f_w = jnp.pad(f_w, ((0, 0), (0, LANES - GLA_GATE_RANK)))
    return jnp.concatenate([q_w, k_w, v_w, r_w, xr_w, gr_w, f_w], axis=1).astype(BF16)


def _hybrid_mix(z, w_fup, b_f, gla_norm, conv_w, conv_b, w_a, b_a, w_x, b_x, lam, batch):
    t = z.shape[0]
    wf = jnp.pad(w_fup, ((0, LANES - GLA_GATE_RANK), (0, 0))).astype(BF16)
    mix = _hybrid_core(
        z.reshape(batch, t // batch, HYB_ZW), wf, b_f.reshape(1, GLA_KW), gla_norm.reshape(1, GLA_VW),
        conv_w, conv_b.reshape(1, LRU_WIDTH), _block_diag_halves(w_a).astype(BF16), b_a.reshape(1, LRU_WIDTH),
        _block_diag_halves(w_x).astype(BF16), b_x.reshape(1, LRU_WIDTH), lam.reshape(1, LRU_WIDTH))
    return mix.reshape(t, GLA_VW + LRU_WIDTH)


def _t5_bucket_table():
    max_exact = REL_BUCKETS // 2
    dist = np.arange(SWA_BLOCK)
    d = np.maximum(dist, 1).astype(np.float32)
    large = max_exact + (np.log(d / max_exact) / math.log(REL_MAX_DIST / max_exact)
                         * (REL_BUCKETS - max_exact)).astype(np.int32)
    bucket = np.where(dist < max_exact, dist, np.minimum(large, REL_BUCKETS - 1)).astype(np.int32)
    i = np.arange(SWA_BLOCK)[:, None]
    j = np.arange(SWA_BLOCK)[None, :]
    return bucket[(i - j) % SWA_BLOCK]


def _roll_half_lanes(x):
    words = pltpu.bitcast(x, jnp.int32)
    return pltpu.bitcast(pltpu.roll(words, LANES // 2, 1), BF16)


def _swa_body(rel_ref, sink_ref, bkt_ref, q_ref, kvc_ref, kvp_ref, o_ref, bias_ref, kd_ref, vo_ref):
    blk = SWA_BLOCK
    dh = SWA_HEAD_DIM
    tq = SWA_TQ
    first_tile = pl.program_id(1) == 0
    row = lax.broadcasted_iota(jnp.int32, (blk, blk), 0)
    col = lax.broadcasted_iota(jnp.int32, (blk, blk), 1)
    in_cur = col <= row
    lo_half = col < dh

    @pl.when((pl.program_id(0) == 0) & first_tile)
    def _():
        bkt = bkt_ref[...]
        for h in range(SWA_HEADS):
            acc = jnp.zeros((blk, blk), F32)
            for k in range(REL_BUCKETS):
                acc = jnp.where(bkt == k, rel_ref[k, h], acc)
            bias_ref[0, h] = acc
            bias_ref[1, h] = jnp.where(in_cur, acc, MASK_VALUE)
        band_lane = lax.broadcasted_iota(jnp.int32, (tq + blk, LANES), 1)
        ones_lo = jnp.where(band_lane < dh, 1.0, 0.0).astype(BF16)
        for kv in range(SWA_KV_HEADS):
            vo_ref[kv, 0, :, LANES:] = ones_lo
            vo_ref[kv, 1, :, LANES:] = 1.0 - ones_lo

    def build_band(src_ref, r0, rows):
        lo = lax.broadcasted_iota(jnp.int32, (rows, LANES), 1) < dh
        zero = jnp.zeros((rows, LANES), BF16)
        for kv in range(SWA_KV_HEADS):
            c0 = LANES * (kv // 2)
            k_pair = src_ref[:, c0:c0 + LANES]
            v_pair = src_ref[:, SWA_KV_W + c0:SWA_KV_W + c0 + LANES]
            k_rot = _roll_half_lanes(k_pair)
            v_rot = _roll_half_lanes(v_pair)
            k_own, k_other, v_own, v_other = ((k_pair, k_rot, v_pair, v_rot) if kv % 2 == 0
                                              else (k_rot, k_pair, v_rot, v_pair))
            kd_ref[kv, r0:r0 + rows, :] = jnp.where(lo, k_own, k_other)
            vo_ref[kv, 0, r0:r0 + rows, :LANES] = jnp.where(lo, v_own, zero)
            vo_ref[kv, 1, r0:r0 + rows, :LANES] = jnp.where(lo, zero, v_other)

    build_band(kvp_ref, 0, blk)
    build_band(kvc_ref, blk, tq)

    lane = lax.broadcasted_iota(jnp.int32, (1, LANES), 1)
    q_scale = (jnp.where(lane < dh, dh ** -0.5, 0.0).astype(BF16), jnp.where(lane < dh, 0.0, dh ** -0.5).astype(BF16))
    zero_p = jnp.zeros((blk, blk), BF16)

    def sub_block(n, carry):
        r0 = pl.multiple_of(n * blk, blk)
        first = jnp.where(first_tile & (n == 0), 1, 0)
        scores = []
        for kv in range(SWA_KV_HEADS):
            k_band = kd_ref[kv, pl.ds(r0, 2 * blk), :]
            for pr in range(SWA_GROUP // 2):
                q_pair = q_ref[pl.ds(r0, blk), pl.ds(LANES * (2 * kv + pr), LANES)]
                for half in range(2):
                    h = SWA_GROUP * kv + 2 * pr + half
                    s = _dot_nt(q_pair * q_scale[half], k_band)
                    scores.append(jnp.where(in_cur, s[:, blk:], s[:, :blk]) + bias_ref[first, h])
        maxes = [jnp.maximum(jnp.max(s, axis=-1, keepdims=True), sink_ref[h]) for h, s in enumerate(scores)]
        exps = [jnp.exp(s - m).astype(BF16) for s, m in zip(scores, maxes)]
        for kv in range(SWA_KV_HEADS):
            for pr in range(SWA_GROUP // 2):
                h0 = SWA_GROUP * kv + 2 * pr
                acc = None
                for half in range(2):
                    e = exps[h0 + half]
                    p_band = jnp.concatenate([jnp.where(in_cur, zero_p, e), jnp.where(in_cur, e, zero_p)], axis=1)
                    part = _dot(p_band, vo_ref[kv, half, pl.ds(r0, 2 * blk), :])
                    acc = part if acc is None else acc + part
                sink_term = jnp.where(lo_half, jnp.exp(sink_ref[h0] - maxes[h0]),
                                      jnp.exp(sink_ref[h0 + 1] - maxes[h0 + 1]))
                out = acc[:, :LANES] / (acc[:, LANES:] + sink_term)
                o_ref[pl.ds(r0, blk), pl.ds(LANES * (2 * kv + pr), LANES)] = out.astype(o_ref.dtype)
        return carry

    lax.fori_loop(0, tq // blk, sub_block, 0)


def _swa_core(z, rel_bias, sinks):
    b, s, _ = z.shape
    kv_col = SWA_Q_W // (2 * SWA_KV_W)
    blocks_per_tile = SWA_TQ // SWA_BLOCK
    smem = lambda: pl.BlockSpec(memory_space=pltpu.SMEM)
    return pl.pallas_call(
        _swa_body,
        grid=(b, s // SWA_TQ),
        in_specs=[
            smem(), smem(),
            pl.BlockSpec((SWA_BLOCK, SWA_BLOCK), lambda i, j: (0, 0)),
            pl.BlockSpec((None, SWA_TQ, SWA_Q_W), lambda i, j: (i, j, 0)),
            pl.BlockSpec((None, SWA_TQ, 2 * SWA_KV_W), lambda i, j: (i, j, kv_col)),
            pl.BlockSpec((None, SWA_BLOCK, 2 * SWA_KV_W),
                         lambda i, j: (i, jnp.maximum(j * blocks_per_tile - 1, 0), kv_col)),
        ],
        out_specs=pl.BlockSpec((None, SWA_TQ, SWA_Q_W), lambda i, j: (i, j, 0)),
        out_shape=jax.ShapeDtypeStruct((b, s, SWA_Q_W), BF16),
        scratch_shapes=[
            pltpu.VMEM((2, SWA_HEADS, SWA_BLOCK, SWA_BLOCK), F32),
            pltpu.VMEM((SWA_KV_HEADS, SWA_TQ + SWA_BLOCK, LANES), BF16),
            pltpu.VMEM((SWA_KV_HEADS, 2, SWA_TQ + SWA_BLOCK, 2 * LANES), BF16),
        ],
        compiler_params=_params("arbitrary", "arbitrary"),
        name="swa_core",
    )(rel_bias, sinks, jnp.asarray(_t5_bucket_table()), z, z, z)


def _swa_mix(z, rel_bias, sinks, batch):
    t = z.shape[0]
    return _swa_core(z.reshape(batch, t // batch, SWA_Q_W + 2 * SWA_KV_W), rel_bias, sinks).reshape(t, SWA_Q_W)


def kernel(x, p, rel_bias, final_norm, ffn1_norm, ffn1_w_gate, ffn1_w_up, ffn1_w_down, mix_norm, ffn2_norm,
           ffn2_w_gate, ffn2_w_up, ffn2_w_down, ple_norm, ple_w_proj, ple_w_gate, hyb_w_in, hyb_w_out, gla_w_fup,
           gla_b_f, gla_norm, lru_conv_w, lru_conv_b, lru_w_a, lru_b_a, lru_w_x, lru_b_x, lru_lambda, swa_w_qkv,
           swa_b_qkv, swa_w_o, swa_b_o, swa_sinks):
    batch, seq, _ = x.shape
    t = batch * seq
    x = x.reshape(t, D_MODEL)
    p = p.reshape(DEPTH, t, PLE_DIM)
    gains = lambda g: g.reshape(DEPTH, 1, D_MODEL)
    ffn1_f32 = (ffn1_w_gate, ffn1_w_up, ffn1_w_down)
    ffn2_f32 = (ffn2_w_gate, ffn2_w_up, ffn2_w_down)
    g1, g2 = gains(ffn1_norm), gains(ffn2_norm)
    ple = (gains(ple_norm), ple_w_gate.astype(BF16), ple_w_proj.astype(BF16))
    w_ffn = [w[0].astype(BF16) for w in ffn1_f32]
    mix_g = gains(mix_norm)
    final_g = final_norm.reshape(1, D_MODEL)
    no_bias_in = jnp.zeros((1, HYB_ZW), F32)
    no_bias_out = jnp.zeros((1, D_MODEL), F32)
    next_ffn = lambda layer: (ffn1_f32, layer + 1) if layer + 1 < DEPTH else ((), 0)
    for i in range(DEPTH):
        if i % 2 == 0:
            e = i // 2
            x, z, w_ffn = _pre_stage(x, i, (g1, *w_ffn), mix_g, _hybrid_in_weights(hyb_w_in[e]), no_bias_in, F32,
                                     PRE_HYB_TM, (ffn2_f32, i), "pre_hyb")
            mix = _hybrid_mix(z, gla_w_fup[e], gla_b_f[e], gla_norm[e], lru_conv_w[e], lru_conv_b[e], lru_w_a[e],
                              lru_b_a[e], lru_w_x[e], lru_b_x[e], lru_lambda[e], batch)
            x, w_ffn = _post_stage(mix, x, p, i, hyb_w_out[e].astype(BF16), no_bias_out, (g2, *w_ffn), ple, final_g,
                                   next_ffn(i), "post_hyb")
        else:
            o = i // 2
            x, z, w_ffn = _pre_stage(x, i, (g1, *w_ffn), mix_g, swa_w_qkv[o].astype(BF16),
                                     swa_b_qkv[o].reshape(1, -1), BF16, TOK_TM, (ffn2_f32, i), "pre_swa")
            mix = _swa_mix(z, rel_bias, swa_sinks[o], batch)
            x, w_ffn = _post_stage(mix, x, p, i, swa_w_o[o].astype(BF16), swa_b_o[o].reshape(1, D_MODEL),
                                   (g2, *w_ffn), ple, final_g, next_ffn(i), "post_swa")
    return x.reshape(batch, seq, D_MODEL)
```

```python
import functools
import math

import jax
import jax.numpy as jnp
import numpy as np
from jax import lax
from jax.experimental import pallas as pl
from jax.experimental.pallas import tpu as pltpu

F32 = jnp.float32
BF16 = jnp.bfloat16

D_MODEL = 1024
DEPTH = 4
PLE_DIM = 256
D_FF = 2816
EPS = 1e-6

GLA_HEADS = 4
GLA_DK = 64
GLA_DV = 128
GLA_KW = GLA_HEADS * GLA_DK
GLA_VW = GLA_HEADS * GLA_DV
GLA_GATE_RANK = 16
GLA_GATE_TAU = 16.0
GLA_CHUNK = 64
LRU_WIDTH = 512
LRU_BLOCKS = 8
LRU_CONV_W = 4
LRU_C = 8.0

SWA_HEADS = 16
SWA_KV_HEADS = 4
SWA_HEAD_DIM = 64
SWA_GROUP = SWA_HEADS // SWA_KV_HEADS
SWA_BLOCK = 128
SWA_Q_W = SWA_HEADS * SWA_HEAD_DIM
SWA_KV_W = SWA_KV_HEADS * SWA_HEAD_DIM
REL_BUCKETS = 32
REL_MAX_DIST = 128
MASK_VALUE = -1e30

LANES = 128
SUBLANES = 8
BF16_SUBLANES = 16
VMEM_LIMIT_BYTES = 56 * 1024 * 1024

TOK_TM = 512
PRE_HYB_TM = 512
FFN_TF = 256
HYB_TS = 512
GLA_TILE = 256
SWA_TQ = 1024

Z_Q = 0
Z_K = Z_Q + GLA_KW
Z_V = Z_K + GLA_KW
Z_R = Z_V + GLA_VW
Z_X = Z_R + GLA_VW
Z_G = Z_X + LRU_WIDTH
Z_F = Z_G + LRU_WIDTH
HYB_ZW = Z_F + LANES


def _params(*semantics):
    return pltpu.CompilerParams(dimension_semantics=semantics, vmem_limit_bytes=VMEM_LIMIT_BYTES)


def _rms(x, g):
    return x * lax.rsqrt(jnp.mean(x * x, axis=-1, keepdims=True) + EPS) * g


def _sigmoid(x):
    return 1.0 / (1.0 + jnp.exp(-x))


def _dot(a, b):
    return jnp.dot(a, b, preferred_element_type=F32)


def _dot_nt(a, b):
    return lax.dot_general(a, b, (((1,), (1,)), ((), ())), preferred_element_type=F32)


def _dot_tn(a, b):
    return lax.dot_general(a, b, (((0,), (0,)), ((), ())), preferred_element_type=F32)


def _ffn_apply(x, g_ref, wg_ref, wu_ref, wd_ref):
    xn = _rms(x, g_ref[...]).astype(BF16)
    acc = x
    starts = range(0, D_FF, FFN_TF)
    gate_up = lambda c0: (_dot(xn, wg_ref[:, c0:c0 + FFN_TF]), _dot(xn, wu_ref[:, c0:c0 + FFN_TF]))
    nxt = gate_up(starts[0])
    for idx, c0 in enumerate(starts):
        a, b = nxt
        if idx + 1 < len(starts):
            nxt = gate_up(starts[idx + 1])
        h = (0.5 * a) * _sigmoid(a) * b
        acc = acc + _dot(h.astype(BF16), wd_ref[c0:c0 + FFN_TF, :])
    return acc


def _cast_next_weights(refs):
    half = len(refs) // 2
    for src_ref, dst_ref in zip(refs[:half], refs[half:]):
        dst_ref[...] = src_ref[...].astype(BF16)


def _pre_body(x_ref, g1_ref, wg_ref, wu_ref, wd_ref, gm_ref, win_ref, bin_ref, *rest):
    nxt_in, (xo_ref, z_ref), nxt_out = rest[:-5], rest[-5:-3], rest[-3:]
    _cast_next_weights(nxt_in + nxt_out)
    x = _ffn_apply(x_ref[...], g1_ref, wg_ref, wu_ref, wd_ref)
    xo_ref[...] = x
    z_ref[...] = (_dot(_rms(x, gm_ref[...]).astype(BF16), win_ref[...]) + bin_ref[...]).astype(z_ref.dtype)


def _post_body(mix_ref, x_ref, p_ref, wo_ref, bo_ref, g2_ref, wg_ref, wu_ref, wd_ref, gp_ref, wpg_ref, wpp_ref,
               fg_ref, *rest, final):
    n_next = (len(rest) - 1) // 2
    nxt_in, o_ref, nxt_out = rest[:n_next], rest[n_next], rest[n_next + 1:]
    _cast_next_weights(nxt_in + nxt_out)
    x = x_ref[...] + (_dot(mix_ref[...], wo_ref[...]) + bo_ref[...])
    x = _ffn_apply(x, g2_ref, wg_ref, wu_ref, wd_ref)
    gate = _sigmoid(_dot(_rms(x, gp_ref[...]).astype(BF16), wpg_ref[...]))
    x = x + gate * _dot(p_ref[...].astype(BF16), wpp_ref[...])
    if final:
        x = _rms(x, fg_ref[...])
    o_ref[...] = x


def _rows(width, tm=TOK_TM):
    return pl.BlockSpec((tm, width), lambda i: (i, 0))


def _resident(array, layer=None):
    if layer is None:
        return pl.BlockSpec(array.shape, lambda i: (0, 0), pipeline_mode=pl.Buffered(1))
    return pl.BlockSpec((None,) + array.shape[1:], lambda i: (layer, 0, 0), pipeline_mode=pl.Buffered(1))


def _cast_slices(weights, layer, steps):
    in_specs, out_specs, out_shapes = [], [], []
    for w in weights:
        _, rows, cols = w.shape
        span = 1 if (rows // steps) % BF16_SUBLANES == 0 else 2
        block = rows * span // steps
        assert rows % block == 0 and block % BF16_SUBLANES == 0
        in_specs.append(pl.BlockSpec((None, block, cols), lambda i, span=span: (layer, i // span, 0)))
        out_specs.append(pl.BlockSpec((block, cols), lambda i, span=span: (i // span, 0)))
        out_shapes.append(jax.ShapeDtypeStruct((rows, cols), BF16))
    return in_specs, out_specs, out_shapes


def _ffn_specs(ffn, layer):
    return [_resident(ffn[0], layer)] + [_resident(w) for w in ffn[1:]]


def _pre_stage(x, layer, ffn, mix_g, w_in, b_in, z_dtype, tm, next_ffn, name):
    t = x.shape[0]
    n = w_in.shape[1]
    nxt_w, nxt_layer = next_ffn
    cast_in, cast_out, cast_shapes = _cast_slices(nxt_w, nxt_layer, t // tm)
    x, z, *nxt = pl.pallas_call(
        _pre_body,
        grid=(t // tm,),
        in_specs=[_rows(D_MODEL, tm)] + _ffn_specs(ffn, layer)
        + [_resident(mix_g, layer), _resident(w_in), _resident(b_in)] + cast_in,
        out_specs=[_rows(D_MODEL, tm), _rows(n, tm)] + cast_out,
        out_shape=[jax.ShapeDtypeStruct((t, D_MODEL), F32), jax.ShapeDtypeStruct((t, n), z_dtype)] + cast_shapes,
        compiler_params=_params("arbitrary"),
        name=name,
    )(x, *ffn, mix_g, w_in, b_in, *nxt_w)
    return x, z, nxt


def _post_stage(mix, x, p, layer, w_o, b_o, ffn, ple, final_g, next_ffn, name):
    t = x.shape[0]
    nxt_w, nxt_layer = next_ffn
    cast_in, cast_out, cast_shapes = _cast_slices(nxt_w, nxt_layer, t // TOK_TM)
    x, *nxt = pl.pallas_call(
        functools.partial(_post_body, final=(layer == DEPTH - 1)),
        grid=(t // TOK_TM,),
        in_specs=[_rows(mix.shape[1]), _rows(D_MODEL),
                  pl.BlockSpec((None, TOK_TM, PLE_DIM), lambda i: (layer, i, 0)),
                  _resident(w_o), _resident(b_o)]
        + _ffn_specs(ffn, layer) + [_resident(w, layer) for w in ple] + [_resident(final_g)] + cast_in,
        out_specs=[_rows(D_MODEL)] + cast_out,
        out_shape=[jax.ShapeDtypeStruct((t, D_MODEL), F32)] + cast_shapes,
        compiler_params=_params("arbitrary"),
        name=name,
    )(mix, x, p, w_o, b_o, *ffn, *ple, final_g, *nxt_w)
    return x, nxt


def _split3(x):
    hi = x.astype(BF16)
    r1 = x - hi.astype(F32)
    mid = r1.astype(BF16)
    lo = (r1 - mid.astype(F32)).astype(BF16)
    return hi, mid, lo


def _hybrid_body(z_ref, wf_ref, bf_ref, gn_ref, cw_ref, cb_ref, wa_ref, ba_ref, wx_ref, bx_ref, lam_ref,
                 o_ref, st_ref, xbuf_ref, hc_ref):
    ts = HYB_TS
    gt = GLA_TILE
    c_len = GLA_CHUNK
    n_chunks = gt // c_len
    half_w = LRU_WIDTH // 2

    @pl.when(pl.program_id(1) == 0)
    def _():
        st_ref[...] = jnp.zeros_like(st_ref)
        xbuf_ref[0:SUBLANES, :] = jnp.zeros((SUBLANES, LRU_WIDTH), F32)
        hc_ref[...] = jnp.zeros_like(hc_ref)

    xbuf_ref[SUBLANES:SUBLANES + ts, :] = z_ref[:, Z_X:Z_X + LRU_WIDTH]
    taps_back = LRU_CONV_W - 1
    xc = xbuf_ref[SUBLANES - taps_back:SUBLANES - taps_back + ts, :] * cw_ref[0:1, :]
    for tap in range(1, LRU_CONV_W):
        off = SUBLANES - taps_back + tap
        xc = xc + xbuf_ref[off:off + ts, :] * cw_ref[tap:tap + 1, :]
    xc = xc + cb_ref[...]
    xbuf_ref[0:SUBLANES, :] = xbuf_ref[ts:ts + SUBLANES, :]
    xcb = xc.astype(BF16)
    r_pre = [_dot(xcb[:, d * half_w:(d + 1) * half_w], wa_ref[d]) for d in range(2)]
    i_pre = [_dot(xcb[:, d * half_w:(d + 1) * half_w], wx_ref[d]) for d in range(2)]
    neg_lam = -lam_ref[...]
    softplus = jnp.maximum(neg_lam, 0.0) + jnp.log1p(jnp.exp(-jnp.abs(neg_lam)))

    def lru_inputs(r0, rows):
        rs = slice(r0, r0 + rows)
        r_gate = _sigmoid(jnp.concatenate([r_pre[0][rs], r_pre[1][rs]], axis=1) + ba_ref[...])
        i_gate = _sigmoid(jnp.concatenate([i_pre[0][rs], i_pre[1][rs]], axis=1) + bx_ref[...])
        log_a = -LRU_C * r_gate * softplus
        th = jnp.tanh(log_a)
        return jnp.exp(log_a), jnp.sqrt(-2.0 * th / (1.0 - th)) * (i_gate * xc[rs])

    row = lax.broadcasted_iota(jnp.int32, (gt, gt), 0)
    col = lax.broadcasted_iota(jnp.int32, (gt, gt), 1)
    in_chunk_causal = (row // c_len == col // c_len) & (col <= row)
    tri = jnp.where(in_chunk_causal, 1.0, 0.0).astype(BF16)
    lane = lax.broadcasted_iota(jnp.int32, (1, LANES), 1)
    head_lanes = (jnp.where(lane < GLA_DK, 1.0, 0.0).astype(BF16), jnp.where(lane < GLA_DK, 0.0, 1.0).astype(BF16))
    row_chunk = lax.broadcasted_iota(jnp.int32, (gt, LANES), 0) // c_len
    zero_k = jnp.zeros((gt, LANES), BF16)
    a_parts, u_parts = [], []

    def gla_tile(r0):
        rows = slice(r0, r0 + gt)
        f_low = z_ref[rows, Z_F:Z_F + LANES].astype(BF16)
        gate_in = _dot(f_low, wf_ref[...]) + bf_ref[...]
        log_f = (jnp.minimum(gate_in, 0.0) - jnp.log1p(jnp.exp(-jnp.abs(gate_in)))) * (1.0 / GLA_GATE_TAU)
        hi, mid, lo = _split3(log_f)
        b_all = _dot(tri, hi) + _dot(tri, mid) + _dot(tri, lo)
        b_last = [b_all[(c + 1) * c_len - 1:(c + 1) * c_len] for c in range(n_chunks)]
        b_last_rows = jnp.concatenate([jnp.broadcast_to(bl, (c_len, GLA_KW)) for bl in b_last], axis=0)
        decay = [jnp.exp(bl) for bl in b_last]
        q_all = z_ref[rows, Z_Q:Z_Q + GLA_KW]
        k_all = z_ref[rows, Z_K:Z_K + GLA_KW]
        q_dec = (q_all * (GLA_DK ** -0.5) * jnp.exp(b_all)).astype(BF16)
        k_dec = (k_all * jnp.exp(-b_all)).astype(BF16)
        k_end = (k_all * jnp.exp(b_last_rows - b_all)).astype(BF16)
        for pair in range(GLA_HEADS // 2):
            kl = slice(pair * LANES, (pair + 1) * LANES)
            q_pair = q_dec[:, kl]
            q_heads = jnp.concatenate([q_pair * head_lanes[0], q_pair * head_lanes[1]], axis=0)
            scores = _dot_nt(q_heads, k_dec[:, kl])
            k_end_pair = k_end[:, kl]
            k_end_blocks = jnp.concatenate(
                [jnp.where(row_chunk == c, k_end_pair, zero_k) for c in range(n_chunks)], axis=1)
            for half in range(2):
                hd = 2 * pair + half
                vl = slice(hd * GLA_DV, (hd + 1) * GLA_DV)
                q_h = q_heads[half * gt:(half + 1) * gt]
                att = jnp.where(in_chunk_causal, scores[half * gt:(half + 1) * gt], 0.0).astype(BF16)
                v_h = z_ref[rows, Z_V + hd * GLA_DV:Z_V + (hd + 1) * GLA_DV].astype(BF16)
                o = _dot(att, v_h)
                kv_t = _dot_tn(v_h, k_end_blocks)
                s_t = st_ref[hd]
                o_inter = []
                for c in range(n_chunks):
                    o_inter.append(_dot_nt(q_h[c * c_len:(c + 1) * c_len], s_t.astype(BF16)))
                    s_t = s_t * decay[c][:, kl] + kv_t[:, c * LANES:(c + 1) * LANES]
                st_ref[hd] = s_t
                o = o + jnp.concatenate(o_inter, axis=0)
                o = o * lax.rsqrt(jnp.mean(o * o, axis=-1, keepdims=True) + EPS)
                r_h = z_ref[rows, Z_R + hd * GLA_DV:Z_R + (hd + 1) * GLA_DV]
                o_ref[rows, vl] = (o * gn_ref[:, vl] * (r_h * _sigmoid(r_h))).astype(o_ref.dtype)
                block = gt // GLA_HEADS
                a_blk, u_blk = lru_inputs(r0 + hd * block, block)
                a_parts.append(a_blk)
                u_parts.append(u_blk)

    for r0 in range(0, ts, gt):
        gla_tile(r0)

    a_all = jnp.concatenate(a_parts, axis=0)
    u_all = jnp.concatenate(u_parts, axis=0)
    srow = lax.broadcasted_iota(jnp.int32, (SUBLANES, LRU_WIDTH), 0)
    local = []
    for g in range(ts // SUBLANES):
        a = a_all[g * SUBLANES:(g + 1) * SUBLANES]
        u = u_all[g * SUBLANES:(g + 1) * SUBLANES]
        for s in (1, 2, 4):
            keep = srow >= s
            a_sh = jnp.where(keep, pltpu.roll(a, s, 0), 1.0)
            u_sh = jnp.where(keep, pltpu.roll(u, s, 0), 0.0)
            u = a * u_sh + u
            a = a * a_sh
        local.append((a, u))
    carry = hc_ref[...]
    h_groups = []
    for a, u in local:
        h_groups.append(a * carry + u)
        carry = a[SUBLANES - 1:SUBLANES, :] * carry + u[SUBLANES - 1:SUBLANES, :]
    hc_ref[...] = carry
    g_in = z_ref[:, Z_G:Z_G + LRU_WIDTH]
    gelu = 0.5 * g_in * (1.0 + jnp.tanh(math.sqrt(2.0 / math.pi) * (g_in + 0.044715 * (g_in * g_in * g_in))))
    o_ref[:, GLA_VW:GLA_VW + LRU_WIDTH] = (jnp.concatenate(h_groups, axis=0) * gelu).astype(o_ref.dtype)


def _hybrid_core(z, wf, bf, gn, cw, cb, wa, ba, wx, bx, lam):
    b, s, _ = z.shape
    const = lambda shape: pl.BlockSpec(shape, lambda i, j: (0,) * len(shape))
    return pl.pallas_call(
        _hybrid_body,
        grid=(b, s // HYB_TS),
        in_specs=[
            pl.BlockSpec((None, HYB_TS, HYB_ZW), lambda i, j: (i, j, 0)),
            const((LANES, GLA_KW)), const((1, GLA_KW)), const((1, GLA_VW)),
            const((LRU_CONV_W, LRU_WIDTH)), const((1, LRU_WIDTH)),
            const((2, LRU_WIDTH // 2, LRU_WIDTH // 2)), const((1, LRU_WIDTH)),
            const((2, LRU_WIDTH // 2, LRU_WIDTH // 2)), const((1, LRU_WIDTH)),
            const((1, LRU_WIDTH)),
        ],
        out_specs=pl.BlockSpec((None, HYB_TS, GLA_VW + LRU_WIDTH), lambda i, j: (i, j, 0)),
        out_shape=jax.ShapeDtypeStruct((b, s, GLA_VW + LRU_WIDTH), BF16),
        scratch_shapes=[
            pltpu.VMEM((GLA_HEADS, GLA_DV, LANES), F32),
            pltpu.VMEM((HYB_TS + 2 * SUBLANES, LRU_WIDTH), F32),
            pltpu.VMEM((1, LRU_WIDTH), F32),
        ],
        compiler_params=_params("parallel", "arbitrary"),
        name="hybrid_core",
    )(z, wf, bf, gn, cw, cb, wa, ba, wx, bx, lam)


def _block_diag_halves(w):
    g, bw, _ = w.shape
    eye = jnp.eye(g // 2, dtype=w.dtype)
    w = w.reshape(2, g // 2, bw, bw)
    return (eye[None, :, None, :, None] * w[:, :, :, None, :]).reshape(2, g * bw // 2, g * bw // 2)


def _hybrid_in_weights(w_in):
    q_w, k_w, v_w, r_w, f_w, xr_w, gr_w = jnp.split(
        w_in, np.cumsum([GLA_KW, GLA_KW, GLA_VW, GLA_VW, GLA_GATE_RANK, LRU_WIDTH]).tolist(), axis=1)
    f_w = jnp.pad(f_w, ((0, 0), (0, LANES - GLA_GATE_RANK)))
    return jnp.concatenate([q_w, k_w, v_w, r_w, xr_w, gr_w, f_w], axis=1).astype(BF16)


def _hybrid_mix(z, w_fup, b_f, gla_norm, conv_w, conv_b, w_a, b_a, w_x, b_x, lam, batch):
    t = z.shape[0]
    wf = jnp.pad(w_fup, ((0, LANES - GLA_GATE_RANK), (0, 0))).astype(BF16)
    mix = _hybrid_core(
        z.reshape(batch, t // batch, HYB_ZW), wf, b_f.reshape(1, GLA_KW), gla_norm.reshape(1, GLA_VW),
        conv_w, conv_b.reshape(1, LRU_WIDTH), _block_diag_halves(w_a).astype(BF16), b_a.reshape(1, LRU_WIDTH),
        _block_diag_halves(w_x).astype(BF16), b_x.reshape(1, LRU_WIDTH), lam.reshape(1, LRU_WIDTH))
    return mix.reshape(t, GLA_VW + LRU_WIDTH)


def _t5_bucket_table():
    max_exact = REL_BUCKETS // 2
    dist = np.arange(SWA_BLOCK)
    d = np.maximum(dist, 1).astype(np.float32)
    large = max_exact + (np.log(d / max_exact) / math.log(REL_MAX_DIST / max_exact)
                         * (REL_BUCKETS - max_exact)).astype(np.int32)
    bucket = np.where(dist < max_exact, dist, np.minimum(large, REL_BUCKETS - 1)).astype(np.int32)
    i = np.arange(SWA_BLOCK)[:, None]
    j = np.arange(SWA_BLOCK)[None, :]
    return bucket[(i - j) % SWA_BLOCK]


def _roll_half_lanes(x):
    words = pltpu.bitcast(x, jnp.int32)
    return pltpu.bitcast(pltpu.roll(words, LANES // 2, 1), BF16)


def _swa_body(rel_ref, sink_ref, bkt_ref, q_ref, kvc_ref, kvp_ref, o_ref, bias_ref, kd_ref, vo_ref):
    blk = SWA_BLOCK
    dh = SWA_HEAD_DIM
    tq = SWA_TQ
    first_tile = pl.program_id(1) == 0
    row = lax.broadcasted_iota(jnp.int32, (blk, blk), 0)
    col = lax.broadcasted_iota(jnp.int32, (blk, blk), 1)
    in_cur = col <= row
    lo_half = col < dh

    @pl.when((pl.program_id(0) == 0) & first_tile)
    def _():
        bkt = bkt_ref[...]
        for h in range(SWA_HEADS):
            acc = jnp.zeros((blk, blk), F32)
            for k in range(REL_BUCKETS):
                acc = jnp.where(bkt == k, rel_ref[k, h], acc)
            bias_ref[0, h] = acc
            bias_ref[1, h] = jnp.where(in_cur, acc, MASK_VALUE)
        band_lane = lax.broadcasted_iota(jnp.int32, (tq + blk, LANES), 1)
        ones_lo = jnp.where(band_lane < dh, 1.0, 0.0).astype(BF16)
        for kv in range(SWA_KV_HEADS):
            vo_ref[kv, 0, :, LANES:] = ones_lo
            vo_ref[kv, 1, :, LANES:] = 1.0 - ones_lo

    def build_band(src_ref, r0, rows):
        lo = lax.broadcasted_iota(jnp.int32, (rows, LANES), 1) < dh
        zero = jnp.zeros((rows, LANES), BF16)
        for kv in range(SWA_KV_HEADS):
            c0 = LANES * (kv // 2)
            k_pair = src_ref[:, c0:c0 + LANES]
            v_pair = src_ref[:, SWA_KV_W + c0:SWA_KV_W + c0 + LANES]
            k_rot = _roll_half_lanes(k_pair)
            v_rot = _roll_half_lanes(v_pair)
            k_own, k_other, v_own, v_other = ((k_pair, k_rot, v_pair, v_rot) if kv % 2 == 0
                                              else (k_rot, k_pair, v_rot, v_pair))
            kd_ref[kv, r0:r0 + rows, :] = jnp.where(lo, k_own, k_other)
            vo_ref[kv, 0, r0:r0 + rows, :LANES] = jnp.where(lo, v_own, zero)
            vo_ref[kv, 1, r0:r0 + rows, :LANES] = jnp.where(lo, zero, v_other)

    build_band(kvp_ref, 0, blk)
    build_band(kvc_ref, blk, tq)

    lane = lax.broadcasted_iota(jnp.int32, (1, LANES), 1)
    q_scale = (jnp.where(lane < dh, dh ** -0.5, 0.0).astype(BF16), jnp.where(lane < dh, 0.0, dh ** -0.5).astype(BF16))
    zero_p = jnp.zeros((blk, blk), BF16)

    def sub_block(n, carry):
        r0 = pl.multiple_of(n * blk, blk)
        first = jnp.where(first_tile & (n == 0), 1, 0)
        scores = []
        for kv in range(SWA_KV_HEADS):
            k_band = kd_ref[kv, pl.ds(r0, 2 * blk), :]
            for pr in range(SWA_GROUP // 2):
                q_pair = q_ref[pl.ds(r0, blk), pl.ds(LANES * (2 * kv + pr), LANES)]
                for half in range(2):
                    h = SWA_GROUP * kv + 2 * pr + half
                    s = _dot_nt(q_pair * q_scale[half], k_band)
                    scores.append(jnp.where(in_cur, s[:, blk:], s[:, :blk]) + bias_ref[first, h])
        maxes = [jnp.maximum(jnp.max(s, axis=-1, keepdims=True), sink_ref[h]) for h, s in enumerate(scores)]
        exps = [jnp.exp(s - m).astype(BF16) for s, m in zip(scores, maxes)]
        for kv in range(SWA_KV_HEADS):
            for pr in range(SWA_GROUP // 2):
                h0 = SWA_GROUP * kv + 2 * pr
                acc = None
                for half in range(2):
                    e = exps[h0 + half]
                    p_band = jnp.concatenate([jnp.where(in_cur, zero_p, e), jnp.where(in_cur, e, zero_p)], axis=1)
                    part = _dot(p_band, vo_ref[kv, half, pl.ds(r0, 2 * blk), :])
                    acc = part if acc is None else acc + part
                sink_term = jnp.where(lo_half, jnp.exp(sink_ref[h0] - maxes[h0]),
                                      jnp.exp(sink_ref[h0 + 1] - maxes[h0 + 1]))
                out = acc[:, :LANES] / (acc[:, LANES:] + sink_term)
                o_ref[pl.ds(r0, blk), pl.ds(LANES * (2 * kv + pr), LANES)] = out.astype(o_ref.dtype)
        return carry

    lax.fori_loop(0, tq // blk, sub_block, 0)


def _swa_core(z, rel_bias, sinks):
    b, s, _ = z.shape
    kv_col = SWA_Q_W // (2 * SWA_KV_W)
    blocks_per_tile = SWA_TQ // SWA_BLOCK
    smem = lambda: pl.BlockSpec(memory_space=pltpu.SMEM)
    return pl.pallas_call(
        _swa_body,
        grid=(b, s // SWA_TQ),
        in_specs=[
            smem(), smem(),
            pl.BlockSpec((SWA_BLOCK, SWA_BLOCK), lambda i, j: (0, 0)),
            pl.BlockSpec((None, SWA_TQ, SWA_Q_W), lambda i, j: (i, j, 0)),
            pl.BlockSpec((None, SWA_TQ, 2 * SWA_KV_W), lambda i, j: (i, j, kv_col)),
            pl.BlockSpec((None, SWA_BLOCK, 2 * SWA_KV_W),
                         lambda i, j: (i, jnp.maximum(j * blocks_per_tile - 1, 0), kv_col)),
        ],
        out_specs=pl.BlockSpec((None, SWA_TQ, SWA_Q_W), lambda i, j: (i, j, 0)),
        out_shape=jax.ShapeDtypeStruct((b, s, SWA_Q_W), BF16),
        scratch_shapes=[
            pltpu.VMEM((2, SWA_HEADS, SWA_BLOCK, SWA_BLOCK), F32),
            pltpu.VMEM((SWA_KV_HEADS, SWA_TQ + SWA_BLOCK, LANES), BF16),
            pltpu.VMEM((SWA_KV_HEADS, 2, SWA_TQ + SWA_BLOCK, 2 * LANES), BF16),
        ],
        compiler_params=_params("arbitrary", "arbitrary"),
        name="swa_core",
    )(rel_bias, sinks, jnp.asarray(_t5_bucket_table()), z, z, z)


def _swa_mix(z, rel_bias, sinks, batch):
    t = z.shape[0]
    return _swa_core(z.reshape(batch, t // batch, SWA_Q_W + 2 * SWA_KV_W), rel_bias, sinks).reshape(t, SWA_Q_W)


def kernel(x, p, rel_bias, final_norm, ffn1_norm, ffn1_w_gate, ffn1_w_up, ffn1_w_down, mix_norm, ffn2_norm,
           ffn2_w_gate, ffn2_w_up, ffn2_w_down, ple_norm, ple_w_proj, ple_w_gate, hyb_w_in, hyb_w_out, gla_w_fup,
           gla_b_f, gla_norm, lru_conv_w, lru_conv_b, lru_w_a, lru_b_a, lru_w_x, lru_b_x, lru_lambda, swa_w_qkv,
           swa_b_qkv, swa_w_o, swa_b_o, swa_sinks):
    batch, seq, _ = x.shape
    t = batch * seq
    x = x.reshape(t, D_MODEL)
    p = p.reshape(DEPTH, t, PLE_DIM)
    gains = lambda g: g.reshape(DEPTH, 1, D_MODEL)
    ffn1_f32 = (ffn1_w_gate, ffn1_w_up, ffn1_w_down)
    ffn2_f32 = (ffn2_w_gate, ffn2_w_up, ffn2_w_down)
    g1, g2 = gains(ffn1_norm), gains(ffn2_norm)
    ple = (gains(ple_norm), ple_w_gate.astype(BF16), ple_w_proj.astype(BF16))
    w_ffn = [w[0].astype(BF16) for w in ffn1_f32]
    mix_g = gains(mix_norm)
    final_g = final_norm.reshape(1, D_MODEL)
    no_bias_in = jnp.zeros((1, HYB_ZW), F32)
    no_bias_out = jnp.zeros((1, D_MODEL), F32)
    next_ffn = lambda layer: (ffn1_f32, layer + 1) if layer + 1 < DEPTH else ((), 0)
    for i in range(DEPTH):
        if i % 2 == 0:
            e = i // 2
            x, z, w_ffn = _pre_stage(x, i, (g1, *w_ffn), mix_g, _hybrid_in_weights(hyb_w_in[e]), no_bias_in, F32,
                                     PRE_HYB_TM, (ffn2_f32, i), "pre_hyb")
            mix = _hybrid_mix(z, gla_w_fup[e], gla_b_f[e], gla_norm[e], lru_conv_w[e], lru_conv_b[e], lru_w_a[e],
                              lru_b_a[e], lru_w_x[e], lru_b_x[e], lru_lambda[e], batch)
            x, w_ffn = _post_stage(mix, x, p, i, hyb_w_out[e].astype(BF16), no_bias_out, (g2, *w_ffn), ple, final_g,
                                   next_ffn(i), "post_hyb")
        else:
            o = i // 2
            x, z, w_ffn = _pre_stage(x, i, (g1, *w_ffn), mix_g, swa_w_qkv[o].astype(BF16),
                                     swa_b_qkv[o].reshape(1, -1), BF16, TOK_TM, (ffn2_f32, i), "pre_swa")
            mix = _swa_mix(z, rel_bias, swa_sinks[o], batch)
            x, w_ffn = _post_stage(mix, x, p, i, swa_w_o[o].astype(BF16), swa_b_o[o].reshape(1, D_MODEL),
                                   (g2, *w_ffn), ple, final_g, next_ffn(i), "post_swa")
    return x.reshape(batch, seq, D_MODEL)
```

```python
import functools
import math

import jax
import jax.numpy as jnp
import numpy as np
from jax import lax
from jax.experimental import pallas as pl
from jax.experimental.pallas import tpu as pltpu

F32 = jnp.float32
BF16 = jnp.bfloat16

D_MODEL = 1024
DEPTH = 4
PLE_DIM = 256
D_FF = 2816
EPS = 1e-6

GLA_HEADS = 4
GLA_DK = 64
GLA_DV = 128
GLA_KW = GLA_HEADS * GLA_DK
GLA_VW = GLA_HEADS * GLA_DV
GLA_GATE_RANK = 16
GLA_GATE_TAU = 16.0
GLA_CHUNK = 64
LRU_WIDTH = 512
LRU_BLOCKS = 8
LRU_CONV_W = 4
LRU_C = 8.0

SWA_HEADS = 16
SWA_KV_HEADS = 4
SWA_HEAD_DIM = 64
SWA_GROUP = SWA_HEADS // SWA_KV_HEADS
SWA_BLOCK = 128
SWA_Q_W = SWA_HEADS * SWA_HEAD_DIM
SWA_KV_W = SWA_KV_HEADS * SWA_HEAD_DIM
REL_BUCKETS = 32
REL_MAX_DIST = 128
MASK_VALUE = -1e30

LANES = 128
SUBLANES = 8
BF16_SUBLANES = 16
VMEM_LIMIT_BYTES = 56 * 1024 * 1024

TOK_TM = 512
PRE_HYB_TM = 512
FFN_TF = 256
HYB_TS = 512
GLA_TILE = 256
SWA_TQ = 1024

Z_Q = 0
Z_K = Z_Q + GLA_KW
Z_V = Z_K + GLA_KW
Z_R = Z_V + GLA_VW
Z_X = Z_R + GLA_VW
Z_G = Z_X + LRU_WIDTH
Z_F = Z_G + LRU_WIDTH
HYB_ZW = Z_F + LANES


def _params(*semantics):
    return pltpu.CompilerParams(dimension_semantics=semantics, vmem_limit_bytes=VMEM_LIMIT_BYTES)


def _rms(x, g):
    return x * lax.rsqrt(jnp.mean(x * x, axis=-1, keepdims=True) + EPS) * g


def _sigmoid(x):
    return 1.0 / (1.0 + jnp.exp(-x))


def _dot(a, b):
    return jnp.dot(a, b, preferred_element_type=F32)


def _dot_nt(a, b):
    return lax.dot_general(a, b, (((1,), (1,)), ((), ())), preferred_element_type=F32)


def _dot_tn(a, b):
    return lax.dot_general(a, b, (((0,), (0,)), ((), ())), preferred_element_type=F32)


def _ffn_apply(x, g_ref, wg_ref, wu_ref, wd_ref):
    xn = _rms(x, g_ref[...]).astype(BF16)
    acc = x
    starts = range(0, D_FF, FFN_TF)
    gate_up = lambda c0: (_dot(xn, wg_ref[:, c0:c0 + FFN_TF]), _dot(xn, wu_ref[:, c0:c0 + FFN_TF]))
    nxt = gate_up(starts[0])
    for idx, c0 in enumerate(starts):
        a, b = nxt
        if idx + 1 < len(starts):
            nxt = gate_up(starts[idx + 1])
        h = (0.5 * a) * _sigmoid(a) * b
        acc = acc + _dot(h.astype(BF16), wd_ref[c0:c0 + FFN_TF, :])
    return acc


def _cast_next_weights(refs):
    half = len(refs) // 2
    for src_ref, dst_ref in zip(refs[:half], refs[half:]):
        dst_ref[...] = src_ref[...].astype(BF16)


def _pre_body(x_ref, g1_ref, wg_ref, wu_ref, wd_ref, gm_ref, win_ref, bin_ref, *rest):
    nxt_in, (xo_ref, z_ref), nxt_out = rest[:-5], rest[-5:-3], rest[-3:]
    _cast_next_weights(nxt_in + nxt_out)
    x = _ffn_apply(x_ref[...], g1_ref, wg_ref, wu_ref, wd_ref)
    xo_ref[...] = x
    z_ref[...] = (_dot(_rms(x, gm_ref[...]).astype(BF16), win_ref[...]) + bin_ref[...]).astype(z_ref.dtype)


def _post_body(mix_ref, x_ref, p_ref, wo_ref, bo_ref, g2_ref, wg_ref, wu_ref, wd_ref, gp_ref, wpg_ref, wpp_ref,
               fg_ref, *rest, final):
    n_next = (len(rest) - 1) // 2
    nxt_in, o_ref, nxt_out = rest[:n_next], rest[n_next], rest[n_next + 1:]
    _cast_next_weights(nxt_in + nxt_out)
    x = x_ref[...] + (_dot(mix_ref[...], wo_ref[...]) + bo_ref[...])
    x = _ffn_apply(x, g2_ref, wg_ref, wu_ref, wd_ref)
    gate = _sigmoid(_dot(_rms(x, gp_ref[...]).astype(BF16), wpg_ref[...]))
    x = x + gate * _dot(p_ref[...].astype(BF16), wpp_ref[...])
    if final:
        x = _rms(x, fg_ref[...])
    o_ref[...] = x


def _rows(width, tm=TOK_TM):
    return pl.BlockSpec((tm, width), lambda i: (i, 0))


def _resident(array, layer=None):
    if layer is None:
        return pl.BlockSpec(array.shape, lambda i: (0, 0), pipeline_mode=pl.Buffered(1))
    return pl.BlockSpec((None,) + array.shape[1:], lambda i: (layer, 0, 0), pipeline_mode=pl.Buffered(1))


def _cast_slices(weights, layer, steps):
    in_specs, out_specs, out_shapes = [], [], []
    for w in weights:
        _, rows, cols = w.shape
        span = 1 if (rows // steps) % BF16_SUBLANES == 0 else 2
        block = rows * span // steps
        assert rows % block == 0 and block % BF16_SUBLANES == 0
        in_specs.append(pl.BlockSpec((None, block, cols), lambda i, span=span: (layer, i // span, 0)))
        out_specs.append(pl.BlockSpec((block, cols), lambda i, span=span: (i // span, 0)))
        out_shapes.append(jax.ShapeDtypeStruct((rows, cols), BF16))
    return in_specs, out_specs, out_shapes


def _ffn_specs(ffn, layer):
    return [_resident(ffn[0], layer)] + [_resident(w) for w in ffn[1:]]


def _pre_stage(x, layer, ffn, mix_g, w_in, b_in, mixer, z_dtype, tm, next_ffn, name):
    t = x.shape[0]
    n = w_in.shape[-1]
    nxt_w, nxt_layer = next_ffn
    cast_in, cast_out, cast_shapes = _cast_slices(nxt_w, nxt_layer, t // tm)
    x, z, *nxt = pl.pallas_call(
        _pre_body,
        grid=(t // tm,),
        in_specs=[_rows(D_MODEL, tm)] + _ffn_specs(ffn, layer)
        + [_resident(mix_g, layer), _resident(w_in, mixer), _resident(b_in, mixer)] + cast_in,
        out_specs=[_rows(D_MODEL, tm), _rows(n, tm)] + cast_out,
        out_shape=[jax.ShapeDtypeStruct((t, D_MODEL), F32), jax.ShapeDtypeStruct((t, n), z_dtype)] + cast_shapes,
        compiler_params=_params("arbitrary"),
        name=name,
    )(x, *ffn, mix_g, w_in, b_in, *nxt_w)
    return x, z, nxt


def _post_stage(mix, x, p, layer, w_o, b_o, mixer, ffn, ple, final_g, next_ffn, name):
    t = x.shape[0]
    nxt_w, nxt_layer = next_ffn
    cast_in, cast_out, cast_shapes = _cast_slices(nxt_w, nxt_layer, t // TOK_TM)
    x, *nxt = pl.pallas_call(
        functools.partial(_post_body, final=(layer == DEPTH - 1)),
        grid=(t // TOK_TM,),
        in_specs=[_rows(mix.shape[1]), _rows(D_MODEL),
                  pl.BlockSpec((None, TOK_TM, PLE_DIM), lambda i: (layer, i, 0)),
                  _resident(w_o, mixer), _resident(b_o, mixer)]
        + _ffn_specs(ffn, layer) + [_resident(w, layer) for w in ple] + [_resident(final_g)] + cast_in,
        out_specs=[_rows(D_MODEL)] + cast_out,
        out_shape=[jax.ShapeDtypeStruct((t, D_MODEL), F32)] + cast_shapes,
        compiler_params=_params("arbitrary"),
        name=name,
    )(mix, x, p, w_o, b_o, *ffn, *ple, final_g, *nxt_w)
    return x, nxt


def _split3(x):
    hi = x.astype(BF16)
    r1 = x - hi.astype(F32)
    mid = r1.astype(BF16)
    lo = (r1 - mid.astype(F32)).astype(BF16)
    return hi, mid, lo


def _hybrid_body(z_ref, wf_ref, bf_ref, gn_ref, cw_ref, cb_ref, wa_ref, ba_ref, wx_ref, bx_ref, lam_ref,
                 o_ref, st_ref, xbuf_ref, hc_ref):
    ts = HYB_TS
    gt = GLA_TILE
    c_len = GLA_CHUNK
    n_chunks = gt // c_len
    half_w = LRU_WIDTH // 2

    @pl.when(pl.program_id(1) == 0)
    def _():
        st_ref[...] = jnp.zeros_like(st_ref)
        xbuf_ref[0:SUBLANES, :] = jnp.zeros((SUBLANES, LRU_WIDTH), F32)
        hc_ref[...] = jnp.zeros_like(hc_ref)

    xbuf_ref[SUBLANES:SUBLANES + ts, :] = z_ref[:, Z_X:Z_X + LRU_WIDTH]
    taps_back = LRU_CONV_W - 1
    xc = xbuf_ref[SUBLANES - taps_back:SUBLANES - taps_back + ts, :] * cw_ref[0:1, :]
    for tap in range(1, LRU_CONV_W):
        off = SUBLANES - taps_back + tap
        xc = xc + xbuf_ref[off:off + ts, :] * cw_ref[tap:tap + 1, :]
    xc = xc + cb_ref[...]
    xbuf_ref[0:SUBLANES, :] = xbuf_ref[ts:ts + SUBLANES, :]
    xcb = xc.astype(BF16)
    r_pre = [_dot(xcb[:, d * half_w:(d + 1) * half_w], wa_ref[d]) for d in range(2)]
    i_pre = [_dot(xcb[:, d * half_w:(d + 1) * half_w], wx_ref[d]) for d in range(2)]
    neg_lam = -lam_ref[...]
    softplus = jnp.maximum(neg_lam, 0.0) + jnp.log1p(jnp.exp(-jnp.abs(neg_lam)))

    def lru_inputs(r0, rows):
        rs = slice(r0, r0 + rows)
        r_gate = _sigmoid(jnp.concatenate([r_pre[0][rs], r_pre[1][rs]], axis=1) + ba_ref[...])
        i_gate = _sigmoid(jnp.concatenate([i_pre[0][rs], i_pre[1][rs]], axis=1) + bx_ref[...])
        log_a = -LRU_C * r_gate * softplus
        th = jnp.tanh(log_a)
        return jnp.exp(log_a), jnp.sqrt(-2.0 * th / (1.0 - th)) * (i_gate * xc[rs])

    row = lax.broadcasted_iota(jnp.int32, (gt, gt), 0)
    col = lax.broadcasted_iota(jnp.int32, (gt, gt), 1)
    in_chunk_causal = (row // c_len == col // c_len) & (col <= row)
    tri = jnp.where(in_chunk_causal, 1.0, 0.0).astype(BF16)
    lane = lax.broadcasted_iota(jnp.int32, (1, LANES), 1)
    head_lanes = (jnp.where(lane < GLA_DK, 1.0, 0.0).astype(BF16), jnp.where(lane < GLA_DK, 0.0, 1.0).astype(BF16))
    row_chunk = lax.broadcasted_iota(jnp.int32, (gt, LANES), 0) // c_len
    zero_k = jnp.zeros((gt, LANES), BF16)
    a_parts, u_parts = [], []

    def gla_tile(r0):
        rows = slice(r0, r0 + gt)
        f_low = z_ref[rows, Z_F:Z_F + LANES].astype(BF16)
        gate_in = _dot(f_low, wf_ref[...]) + bf_ref[...]
        log_f = (jnp.minimum(gate_in, 0.0) - jnp.log1p(jnp.exp(-jnp.abs(gate_in)))) * (1.0 / GLA_GATE_TAU)
        hi, mid, lo = _split3(log_f)
        b_all = _dot(tri, hi) + _dot(tri, mid) + _dot(tri, lo)
        b_last = [b_all[(c + 1) * c_len - 1:(c + 1) * c_len] for c in range(n_chunks)]
        b_last_rows = jnp.concatenate([jnp.broadcast_to(bl, (c_len, GLA_KW)) for bl in b_last], axis=0)
        decay = [jnp.exp(bl) for bl in b_last]
        q_all = z_ref[rows, Z_Q:Z_Q + GLA_KW]
        k_all = z_ref[rows, Z_K:Z_K + GLA_KW]
        q_dec = (q_all * (GLA_DK ** -0.5) * jnp.exp(b_all)).astype(BF16)
        k_dec = (k_all * jnp.exp(-b_all)).astype(BF16)
        k_end = (k_all * jnp.exp(b_last_rows - b_all)).astype(BF16)
        for pair in range(GLA_HEADS // 2):
            kl = slice(pair * LANES, (pair + 1) * LANES)
            q_pair = q_dec[:, kl]
            q_heads = jnp.concatenate([q_pair * head_lanes[0], q_pair * head_lanes[1]], axis=0)
            scores = _dot_nt(q_heads, k_dec[:, kl])
            k_end_pair = k_end[:, kl]
            k_end_blocks = jnp.concatenate(
                [jnp.where(row_chunk == c, k_end_pair, zero_k) for c in range(n_chunks)], axis=1)
            for half in range(2):
                hd = 2 * pair + half
                vl = slice(hd * GLA_DV, (hd + 1) * GLA_DV)
                q_h = q_heads[half * gt:(half + 1) * gt]
                att = jnp.where(in_chunk_causal, scores[half * gt:(half + 1) * gt], 0.0).astype(BF16)
                v_h = z_ref[rows, Z_V + hd * GLA_DV:Z_V + (hd + 1) * GLA_DV].astype(BF16)
                o = _dot(att, v_h)
                kv_t = _dot_tn(v_h, k_end_blocks)
                s_t = st_ref[hd]
                o_inter = []
                for c in range(n_chunks):
                    o_inter.append(_dot_nt(q_h[c * c_len:(c + 1) * c_len], s_t.astype(BF16)))
                    s_t = s_t * decay[c][:, kl] + kv_t[:, c * LANES:(c + 1) * LANES]
                st_ref[hd] = s_t
                o = o + jnp.concatenate(o_inter, axis=0)
                o = o * lax.rsqrt(jnp.mean(o * o, axis=-1, keepdims=True) + EPS)
                r_h = z_ref[rows, Z_R + hd * GLA_DV:Z_R + (hd + 1) * GLA_DV]
                o_ref[rows, vl] = (o * gn_ref[:, vl] * (r_h * _sigmoid(r_h))).astype(o_ref.dtype)
                block = gt // GLA_HEADS
                a_blk, u_blk = lru_inputs(r0 + hd * block, block)
                a_parts.append(a_blk)
                u_parts.append(u_blk)

    for r0 in range(0, ts, gt):
        gla_tile(r0)

    a_all = jnp.concatenate(a_parts, axis=0)
    u_all = jnp.concatenate(u_parts, axis=0)
    srow = lax.broadcasted_iota(jnp.int32, (SUBLANES, LRU_WIDTH), 0)
    local = []
    for g in range(ts // SUBLANES):
        a = a_all[g * SUBLANES:(g + 1) * SUBLANES]
        u = u_all[g * SUBLANES:(g + 1) * SUBLANES]
        for s in (1, 2, 4):
            keep = srow >= s
            a_sh = jnp.where(keep, pltpu.roll(a, s, 0), 1.0)
            u_sh = jnp.where(keep, pltpu.roll(u, s, 0), 0.0)
            u = a * u_sh + u
            a = a * a_sh
        local.append((a, u))
    carry = hc_ref[...]
    h_groups = []
    for a, u in local:
        h_groups.append(a * carry + u)
        carry = a[SUBLANES - 1:SUBLANES, :] * carry + u[SUBLANES - 1:SUBLANES, :]
    hc_ref[...] = carry
    g_in = z_ref[:, Z_G:Z_G + LRU_WIDTH]
    gelu = 0.5 * g_in * (1.0 + jnp.tanh(math.sqrt(2.0 / math.pi) * (g_in + 0.044715 * (g_in * g_in * g_in))))
    o_ref[:, GLA_VW:GLA_VW + LRU_WIDTH] = (jnp.concatenate(h_groups, axis=0) * gelu).astype(o_ref.dtype)


def _hybrid_core(z, wf, bf, gn, cw, cb, wa, ba, wx, bx, lam):
    b, s, _ = z.shape
    const = lambda shape: pl.BlockSpec(shape, lambda i, j: (0,) * len(shape))
    return pl.pallas_call(
        _hybrid_body,
        grid=(b, s // HYB_TS),
        in_specs=[
            pl.BlockSpec((None, HYB_TS, HYB_ZW), lambda i, j: (i, j, 0)),
            const((LANES, GLA_KW)), const((1, GLA_KW)), const((1, GLA_VW)),
            const((LRU_CONV_W, LRU_WIDTH)), const((1, LRU_WIDTH)),
            const((2, LRU_WIDTH // 2, LRU_WIDTH // 2)), const((1, LRU_WIDTH)),
            const((2, LRU_WIDTH // 2, LRU_WIDTH // 2)), const((1, LRU_WIDTH)),
            const((1, LRU_WIDTH)),
        ],
        out_specs=pl.BlockSpec((None, HYB_TS, GLA_VW + LRU_WIDTH), lambda i, j: (i, j, 0)),
        out_shape=jax.ShapeDtypeStruct((b, s, GLA_VW + LRU_WIDTH), BF16),
        scratch_shapes=[
            pltpu.VMEM((GLA_HEADS, GLA_DV, LANES), F32),
            pltpu.VMEM((HYB_TS + 2 * SUBLANES, LRU_WIDTH), F32),
            pltpu.VMEM((1, LRU_WIDTH), F32),
        ],
        compiler_params=_params("parallel", "arbitrary"),
        name="hybrid_core",
    )(z, wf, bf, gn, cw, cb, wa, ba, wx, bx, lam)


def _block_diag_halves(w):
    g, bw, _ = w.shape
    eye = jnp.eye(g // 2, dtype=w.dtype)
    w = w.reshape(2, g // 2, bw, bw)
    return (eye[None, :, None, :, None] * w[:, :, :, None, :]).reshape(2, g * bw // 2, g * bw // 2)


def _hybrid_in_weights(w_in):
    q_w, k_w, v_w, r_w, f_w, xr_w, gr_w = jnp.split(
        w_in, np.cumsum([GLA_KW, GLA_KW, GLA_VW, GLA_VW, GLA_GATE_RANK, LRU_WIDTH]).tolist(), axis=-1)
    f_w = jnp.pad(f_w, ((0, 0), (0, 0), (0, LANES - GLA_GATE_RANK)))
    return jnp.concatenate([q_w, k_w, v_w, r_w, xr_w, gr_w, f_w], axis=-1).astype(BF16)


def _hybrid_mix(z, w_fup, b_f, gla_norm, conv_w, conv_b, w_a, b_a, w_x, b_x, lam, batch):
    t = z.shape[0]
    wf = jnp.pad(w_fup, ((0, LANES - GLA_GATE_RANK), (0, 0))).astype(BF16)
    mix = _hybrid_core(
        z.reshape(batch, t // batch, HYB_ZW), wf, b_f.reshape(1, GLA_KW), gla_norm.reshape(1, GLA_VW),
        conv_w, conv_b.reshape(1, LRU_WIDTH), _block_diag_halves(w_a).astype(BF16), b_a.reshape(1, LRU_WIDTH),
        _block_diag_halves(w_x).astype(BF16), b_x.reshape(1, LRU_WIDTH), lam.reshape(1, LRU_WIDTH))
    return mix.reshape(t, GLA_VW + LRU_WIDTH)


def _t5_bucket_table():
    max_exact = REL_BUCKETS // 2
    dist = np.arange(SWA_BLOCK)
    d = np.maximum(dist, 1).astype(np.float32)
    large = max_exact + (np.log(d / max_exact) / math.log(REL_MAX_DIST / max_exact)
                         * (REL_BUCKETS - max_exact)).astype(np.int32)
    bucket = np.where(dist < max_exact, dist, np.minimum(large, REL_BUCKETS - 1)).astype(np.int32)
    i = np.arange(SWA_BLOCK)[:, None]
    j = np.arange(SWA_BLOCK)[None, :]
    return bucket[(i - j) % SWA_BLOCK]


def _roll_half_lanes(x):
    words = pltpu.bitcast(x, jnp.int32)
    return pltpu.bitcast(pltpu.roll(words, LANES // 2, 1), BF16)


def _swa_body(rel_ref, sink_ref, bkt_ref, q_ref, kvc_ref, kvp_ref, o_ref, bias_ref, kd_ref, vo_ref):
    blk = SWA_BLOCK
    dh = SWA_HEAD_DIM
    tq = SWA_TQ
    first_tile = pl.program_id(1) == 0
    row = lax.broadcasted_iota(jnp.int32, (blk, blk), 0)
    col = lax.broadcasted_iota(jnp.int32, (blk, blk), 1)
    in_cur = col <= row
    lo_half = col < dh

    @pl.when((pl.program_id(0) == 0) & first_tile)
    def _():
        bkt = bkt_ref[...]
        for h in range(SWA_HEADS):
            acc = jnp.zeros((blk, blk), F32)
            for k in range(REL_BUCKETS):
                acc = jnp.where(bkt == k, rel_ref[k, h], acc)
            bias_ref[0, h] = acc
            bias_ref[1, h] = jnp.where(in_cur, acc, MASK_VALUE)
        band_lane = lax.broadcasted_iota(jnp.int32, (tq + blk, LANES), 1)
        ones_lo = jnp.where(band_lane < dh, 1.0, 0.0).astype(BF16)
        for kv in range(SWA_KV_HEADS):
            vo_ref[kv, 0, :, LANES:] = ones_lo
            vo_ref[kv, 1, :, LANES:] = 1.0 - ones_lo

    def build_band(src_ref, r0, rows):
        lo = lax.broadcasted_iota(jnp.int32, (rows, LANES), 1) < dh
        zero = jnp.zeros((rows, LANES), BF16)
        for kv in range(SWA_KV_HEADS):
            c0 = LANES * (kv // 2)
            k_pair = src_ref[:, c0:c0 + LANES]
            v_pair = src_ref[:, SWA_KV_W + c0:SWA_KV_W + c0 + LANES]
            k_rot = _roll_half_lanes(k_pair)
            v_rot = _roll_half_lanes(v_pair)
            k_own, k_other, v_own, v_other = ((k_pair, k_rot, v_pair, v_rot) if kv % 2 == 0
                                              else (k_rot, k_pair, v_rot, v_pair))
            kd_ref[kv, r0:r0 + rows, :] = jnp.where(lo, k_own, k_other)
            vo_ref[kv, 0, r0:r0 + rows, :LANES] = jnp.where(lo, v_own, zero)
            vo_ref[kv, 1, r0:r0 + rows, :LANES] = jnp.where(lo, zero, v_other)

    build_band(kvp_ref, 0, blk)
    build_band(kvc_ref, blk, tq)

    lane = lax.broadcasted_iota(jnp.int32, (1, LANES), 1)
    q_scale = (jnp.where(lane < dh, dh ** -0.5, 0.0).astype(BF16), jnp.where(lane < dh, 0.0, dh ** -0.5).astype(BF16))
    zero_p = jnp.zeros((blk, blk), BF16)

    def sub_block(n, carry):
        r0 = pl.multiple_of(n * blk, blk)
        first = jnp.where(first_tile & (n == 0), 1, 0)
        scores = []
        for kv in range(SWA_KV_HEADS):
            k_band = kd_ref[kv, pl.ds(r0, 2 * blk), :]
            for pr in range(SWA_GROUP // 2):
                q_pair = q_ref[pl.ds(r0, blk), pl.ds(LANES * (2 * kv + pr), LANES)]
                for half in range(2):
                    h = SWA_GROUP * kv + 2 * pr + half
                    s = _dot_nt(q_pair * q_scale[half], k_band)
                    scores.append(jnp.where(in_cur, s[:, blk:], s[:, :blk]) + bias_ref[first, h])
        maxes = [jnp.maximum(jnp.max(s, axis=-1, keepdims=True), sink_ref[h]) for h, s in enumerate(scores)]
        exps = [jnp.exp(s - m).astype(BF16) for s, m in zip(scores, maxes)]
        for kv in range(SWA_KV_HEADS):
            for pr in range(SWA_GROUP // 2):
                h0 = SWA_GROUP * kv + 2 * pr
                acc = None
                for half in range(2):
                    e = exps[h0 + half]
                    p_band = jnp.concatenate([jnp.where(in_cur, zero_p, e), jnp.where(in_cur, e, zero_p)], axis=1)
                    part = _dot(p_band, vo_ref[kv, half, pl.ds(r0, 2 * blk), :])
                    acc = part if acc is None else acc + part
                sink_term = jnp.where(lo_half, jnp.exp(sink_ref[h0] - maxes[h0]),
                                      jnp.exp(sink_ref[h0 + 1] - maxes[h0 + 1]))
                out = acc[:, :LANES] / (acc[:, LANES:] + sink_term)
                o_ref[pl.ds(r0, blk), pl.ds(LANES * (2 * kv + pr), LANES)] = out.astype(o_ref.dtype)
        return carry

    lax.fori_loop(0, tq // blk, sub_block, 0)


def _swa_core(z, rel_bias, sinks):
    b, s, _ = z.shape
    kv_col = SWA_Q_W // (2 * SWA_KV_W)
    blocks_per_tile = SWA_TQ // SWA_BLOCK
    smem = lambda: pl.BlockSpec(memory_space=pltpu.SMEM)
    return pl.pallas_call(
        _swa_body,
        grid=(b, s // SWA_TQ),
        in_specs=[
            smem(), smem(),
            pl.BlockSpec((SWA_BLOCK, SWA_BLOCK), lambda i, j: (0, 0)),
            pl.BlockSpec((None, SWA_TQ, SWA_Q_W), lambda i, j: (i, j, 0)),
            pl.BlockSpec((None, SWA_TQ, 2 * SWA_KV_W), lambda i, j: (i, j, kv_col)),
            pl.BlockSpec((None, SWA_BLOCK, 2 * SWA_KV_W),
                         lambda i, j: (i, jnp.maximum(j * blocks_per_tile - 1, 0), kv_col)),
        ],
        out_specs=pl.BlockSpec((None, SWA_TQ, SWA_Q_W), lambda i, j: (i, j, 0)),
        out_shape=jax.ShapeDtypeStruct((b, s, SWA_Q_W), BF16),
        scratch_shapes=[
            pltpu.VMEM((2, SWA_HEADS, SWA_BLOCK, SWA_BLOCK), F32),
            pltpu.VMEM((SWA_KV_HEADS, SWA_TQ + SWA_BLOCK, LANES), BF16),
            pltpu.VMEM((SWA_KV_HEADS, 2, SWA_TQ + SWA_BLOCK, 2 * LANES), BF16),
        ],
        compiler_params=_params("arbitrary", "arbitrary"),
        name="swa_core",
    )(rel_bias, sinks, jnp.asarray(_t5_bucket_table()), z, z, z)


def _swa_mix(z, rel_bias, sinks, batch):
    t = z.shape[0]
    return _swa_core(z.reshape(batch, t // batch, SWA_Q_W + 2 * SWA_KV_W), rel_bias, sinks).reshape(t, SWA_Q_W)


def kernel(x, p, rel_bias, final_norm, ffn1_norm, ffn1_w_gate, ffn1_w_up, ffn1_w_down, mix_norm, ffn2_norm,
           ffn2_w_gate, ffn2_w_up, ffn2_w_down, ple_norm, ple_w_proj, ple_w_gate, hyb_w_in, hyb_w_out, gla_w_fup,
           gla_b_f, gla_norm, lru_conv_w, lru_conv_b, lru_w_a, lru_b_a, lru_w_x, lru_b_x, lru_lambda, swa_w_qkv,
           swa_b_qkv, swa_w_o, swa_b_o, swa_sinks):
    batch, seq, _ = x.shape
    t = batch * seq
    x = x.reshape(t, D_MODEL)
    p = p.reshape(DEPTH, t, PLE_DIM)
    gains = lambda g: g.reshape(DEPTH, 1, D_MODEL)
    ffn1_f32 = (ffn1_w_gate, ffn1_w_up, ffn1_w_down)
    ffn2_f32 = (ffn2_w_gate, ffn2_w_up, ffn2_w_down)
    g1, g2 = gains(ffn1_norm), gains(ffn2_norm)
    ple = (gains(ple_norm), ple_w_gate.astype(BF16), ple_w_proj.astype(BF16))
    w_ffn = [w[0].astype(BF16) for w in ffn1_f32]
    mix_g = gains(mix_norm)
    final_g = final_norm.reshape(1, D_MODEL)
    hyb_in, hyb_out = _hybrid_in_weights(hyb_w_in), hyb_w_out.astype(BF16)
    n_hyb = hyb_w_in.shape[0]
    hyb_b_in, hyb_b_out = jnp.zeros((n_hyb, 1, HYB_ZW), F32), jnp.zeros((n_hyb, 1, D_MODEL), F32)
    swa_in, swa_out = swa_w_qkv.astype(BF16), swa_w_o.astype(BF16)
    swa_b_in, swa_b_out = swa_b_qkv[:, None, :], swa_b_o[:, None, :]
    next_ffn = lambda layer: (ffn1_f32, layer + 1) if layer + 1 < DEPTH else ((), 0)
    for i in range(DEPTH):
        if i % 2 == 0:
            e = i // 2
            x, z, w_ffn = _pre_stage(x, i, (g1, *w_ffn), mix_g, hyb_in, hyb_b_in, e, F32, PRE_HYB_TM,
                                     (ffn2_f32, i), "pre_hyb")
            mix = _hybrid_mix(z, gla_w_fup[e], gla_b_f[e], gla_norm[e], lru_conv_w[e], lru_conv_b[e], lru_w_a[e],
                              lru_b_a[e], lru_w_x[e], lru_b_x[e], lru_lambda[e], batch)
            x, w_ffn = _post_stage(mix, x, p, i, hyb_out, hyb_b_out, e, (g2, *w_ffn), ple, final_g, next_ffn(i),
                                   "post_hyb")
        else:
            o = i // 2
            x, z, w_ffn = _pre_stage(x, i, (g1, *w_ffn), mix_g, swa_in, swa_b_in, o, BF16, TOK_TM, (ffn2_f32, i),
                                     "pre_swa")
            mix = _swa_mix(z, rel_bias, swa_sinks[o], batch)
            x, w_ffn = _post_stage(mix, x, p, i, swa_out, swa_b_out, o, (g2, *w_ffn), ple, final_g, next_ffn(i),
                                   "post_swa")
    return x.reshape(batch, seq, D_MODEL)
```

```python
import functools
import math

import jax
import jax.numpy as jnp
import numpy as np
from jax import lax
from jax.experimental import pallas as pl
from jax.experimental.pallas import tpu as pltpu

F32 = jnp.float32
BF16 = jnp.bfloat16

D_MODEL = 1024
DEPTH = 4
PLE_DIM = 256
D_FF = 2816
EPS = 1e-6

GLA_HEADS = 4
GLA_DK = 64
GLA_DV = 128
GLA_KW = GLA_HEADS * GLA_DK
GLA_VW = GLA_HEADS * GLA_DV
GLA_GATE_RANK = 16
GLA_GATE_TAU = 16.0
GLA_CHUNK = 64
LRU_WIDTH = 512
LRU_BLOCKS = 8
LRU_CONV_W = 4
LRU_C = 8.0

SWA_HEADS = 16
SWA_KV_HEADS = 4
SWA_HEAD_DIM = 64
SWA_GROUP = SWA_HEADS // SWA_KV_HEADS
SWA_BLOCK = 128
SWA_Q_W = SWA_HEADS * SWA_HEAD_DIM
SWA_KV_W = SWA_KV_HEADS * SWA_HEAD_DIM
REL_BUCKETS = 32
REL_MAX_DIST = 128
MASK_VALUE = -1e30

LANES = 128
SUBLANES = 8
BF16_SUBLANES = 16
VMEM_LIMIT_BYTES = 56 * 1024 * 1024

TOK_TM = 512
PRE_HYB_TM = 512
FFN_TF = 256
HYB_TS = 512
GLA_TILE = 256
SWA_TQ = 1024

Z_Q = 0
Z_K = Z_Q + GLA_KW
Z_V = Z_K + GLA_KW
Z_R = Z_V + GLA_VW
Z_X = Z_R + GLA_VW
Z_G = Z_X + LRU_WIDTH
Z_F = Z_G + LRU_WIDTH
HYB_ZW = Z_F + LANES


def _params(*semantics):
    return pltpu.CompilerParams(dimension_semantics=semantics, vmem_limit_bytes=VMEM_LIMIT_BYTES)


def _rms(x, g):
    return x * lax.rsqrt(jnp.mean(x * x, axis=-1, keepdims=True) + EPS) * g


def _sigmoid(x):
    return 1.0 / (1.0 + jnp.exp(-x))


def _dot(a, b):
    return jnp.dot(a, b, preferred_element_type=F32)


def _dot_nt(a, b):
    return lax.dot_general(a, b, (((1,), (1,)), ((), ())), preferred_element_type=F32)


def _dot_tn(a, b):
    return lax.dot_general(a, b, (((0,), (0,)), ((), ())), preferred_element_type=F32)


def _ffn_apply(x, g_ref, wg_ref, wu_ref, wd_ref):
    xn = _rms(x, g_ref[...]).astype(BF16)
    acc = x
    blocks = [(c0, min(c0 + FFN_TF, D_FF)) for c0 in range(0, D_FF, FFN_TF)]
    gate_up = lambda blk: (_dot(xn, wg_ref[:, blk[0]:blk[1]]), _dot(xn, wu_ref[:, blk[0]:blk[1]]))
    nxt = gate_up(blocks[0])
    for idx, (c0, c1) in enumerate(blocks):
        a, b = nxt
        if idx + 1 < len(blocks):
            nxt = gate_up(blocks[idx + 1])
        h = (0.5 * a) * _sigmoid(a) * b
        acc = acc + _dot(h.astype(BF16), wd_ref[c0:c1, :])
    return acc


def _cast_next_weights(refs):
    half = len(refs) // 2
    for src_ref, dst_ref in zip(refs[:half], refs[half:]):
        dst_ref[...] = src_ref[...].astype(BF16)


def _pre_body(x_ref, g1_ref, wg_ref, wu_ref, wd_ref, gm_ref, win_ref, bin_ref, *rest):
    nxt_in, (xo_ref, z_ref), nxt_out = rest[:-5], rest[-5:-3], rest[-3:]
    _cast_next_weights(nxt_in + nxt_out)
    x = _ffn_apply(x_ref[...], g1_ref, wg_ref, wu_ref, wd_ref)
    xo_ref[...] = x
    z_ref[...] = (_dot(_rms(x, gm_ref[...]).astype(BF16), win_ref[...]) + bin_ref[...]).astype(z_ref.dtype)


def _post_body(mix_ref, x_ref, p_ref, wo_ref, bo_ref, g2_ref, wg_ref, wu_ref, wd_ref, gp_ref, wpg_ref, wpp_ref,
               fg_ref, *rest, final):
    n_next = (len(rest) - 1) // 2
    nxt_in, o_ref, nxt_out = rest[:n_next], rest[n_next], rest[n_next + 1:]
    _cast_next_weights(nxt_in + nxt_out)
    x = x_ref[...] + (_dot(mix_ref[...], wo_ref[...]) + bo_ref[...])
    x = _ffn_apply(x, g2_ref, wg_ref, wu_ref, wd_ref)
    gate = _sigmoid(_dot(_rms(x, gp_ref[...]).astype(BF16), wpg_ref[...]))
    x = x + gate * _dot(p_ref[...].astype(BF16), wpp_ref[...])
    if final:
        x = _rms(x, fg_ref[...])
    o_ref[...] = x


def _rows(width, tm=TOK_TM):
    return pl.BlockSpec((tm, width), lambda i: (i, 0))


def _resident(array, layer=None):
    if layer is None:
        return pl.BlockSpec(array.shape, lambda i: (0, 0), pipeline_mode=pl.Buffered(1))
    return pl.BlockSpec((None,) + array.shape[1:], lambda i: (layer, 0, 0), pipeline_mode=pl.Buffered(1))


def _cast_slices(weights, layer, steps):
    in_specs, out_specs, out_shapes = [], [], []
    for w in weights:
        _, rows, cols = w.shape
        span = 1 if (rows // steps) % BF16_SUBLANES == 0 else 2
        block = rows * span // steps
        assert rows % block == 0 and block % BF16_SUBLANES == 0
        in_specs.append(pl.BlockSpec((None, block, cols), lambda i, span=span: (layer, i // span, 0)))
        out_specs.append(pl.BlockSpec((block, cols), lambda i, span=span: (i // span, 0)))
        out_shapes.append(jax.ShapeDtypeStruct((rows, cols), BF16))
    return in_specs, out_specs, out_shapes


def _ffn_specs(ffn, layer):
    return [_resident(ffn[0], layer)] + [_resident(w) for w in ffn[1:]]


def _pre_stage(x, layer, ffn, mix_g, w_in, b_in, mixer, z_dtype, tm, next_ffn, name):
    t = x.shape[0]
    n = w_in.shape[-1]
    nxt_w, nxt_layer = next_ffn
    cast_in, cast_out, cast_shapes = _cast_slices(nxt_w, nxt_layer, t // tm)
    x, z, *nxt = pl.pallas_call(
        _pre_body,
        grid=(t // tm,),
        in_specs=[_rows(D_MODEL, tm)] + _ffn_specs(ffn, layer)
        + [_resident(mix_g, layer), _resident(w_in, mixer), _resident(b_in, mixer)] + cast_in,
        out_specs=[_rows(D_MODEL, tm), _rows(n, tm)] + cast_out,
        out_shape=[jax.ShapeDtypeStruct((t, D_MODEL), F32), jax.ShapeDtypeStruct((t, n), z_dtype)] + cast_shapes,
        compiler_params=_params("arbitrary"),
        name=name,
    )(x, *ffn, mix_g, w_in, b_in, *nxt_w)
    return x, z, nxt


def _post_stage(mix, x, p, layer, w_o, b_o, mixer, ffn, ple, final_g, next_ffn, name):
    t = x.shape[0]
    nxt_w, nxt_layer = next_ffn
    cast_in, cast_out, cast_shapes = _cast_slices(nxt_w, nxt_layer, t // TOK_TM)
    x, *nxt = pl.pallas_call(
        functools.partial(_post_body, final=(layer == DEPTH - 1)),
        grid=(t // TOK_TM,),
        in_specs=[_rows(mix.shape[1]), _rows(D_MODEL),
                  pl.BlockSpec((None, TOK_TM, PLE_DIM), lambda i: (layer, i, 0)),
                  _resident(w_o, mixer), _resident(b_o, mixer)]
        + _ffn_specs(ffn, layer) + [_resident(w, layer) for w in ple] + [_resident(final_g)] + cast_in,
        out_specs=[_rows(D_MODEL)] + cast_out,
        out_shape=[jax.ShapeDtypeStruct((t, D_MODEL), F32)] + cast_shapes,
        compiler_params=_params("arbitrary"),
        name=name,
    )(mix, x, p, w_o, b_o, *ffn, *ple, final_g, *nxt_w)
    return x, nxt


def _split3(x):
    hi = x.astype(BF16)
    r1 = x - hi.astype(F32)
    mid = r1.astype(BF16)
    lo = (r1 - mid.astype(F32)).astype(BF16)
    return hi, mid, lo


def _hybrid_body(z_ref, wf_ref, bf_ref, gn_ref, cw_ref, cb_ref, wa_ref, ba_ref, wx_ref, bx_ref, lam_ref,
                 o_ref, st_ref, xbuf_ref, hc_ref):
    ts = HYB_TS
    gt = GLA_TILE
    c_len = GLA_CHUNK
    n_chunks = gt // c_len
    half_w = LRU_WIDTH // 2

    @pl.when(pl.program_id(1) == 0)
    def _():
        st_ref[...] = jnp.zeros_like(st_ref)
        xbuf_ref[0:SUBLANES, :] = jnp.zeros((SUBLANES, LRU_WIDTH), F32)
        hc_ref[...] = jnp.zeros_like(hc_ref)

    xbuf_ref[SUBLANES:SUBLANES + ts, :] = z_ref[:, Z_X:Z_X + LRU_WIDTH]
    taps_back = LRU_CONV_W - 1
    xc = xbuf_ref[SUBLANES - taps_back:SUBLANES - taps_back + ts, :] * cw_ref[0:1, :]
    for tap in range(1, LRU_CONV_W):
        off = SUBLANES - taps_back + tap
        xc = xc + xbuf_ref[off:off + ts, :] * cw_ref[tap:tap + 1, :]
    xc = xc + cb_ref[...]
    xbuf_ref[0:SUBLANES, :] = xbuf_ref[ts:ts + SUBLANES, :]
    xcb = xc.astype(BF16)
    r_pre = [_dot(xcb[:, d * half_w:(d + 1) * half_w], wa_ref[d]) for d in range(2)]
    i_pre = [_dot(xcb[:, d * half_w:(d + 1) * half_w], wx_ref[d]) for d in range(2)]
    neg_lam = -lam_ref[...]
    softplus = jnp.maximum(neg_lam, 0.0) + jnp.log1p(jnp.exp(-jnp.abs(neg_lam)))

    def lru_inputs(r0, rows):
        rs = slice(r0, r0 + rows)
        r_gate = _sigmoid(jnp.concatenate([r_pre[0][rs], r_pre[1][rs]], axis=1) + ba_ref[...])
        i_gate = _sigmoid(jnp.concatenate([i_pre[0][rs], i_pre[1][rs]], axis=1) + bx_ref[...])
        log_a = -LRU_C * r_gate * softplus
        th = jnp.tanh(log_a)
        return jnp.exp(log_a), jnp.sqrt(-2.0 * th / (1.0 - th)) * (i_gate * xc[rs])

    row = lax.broadcasted_iota(jnp.int32, (gt, gt), 0)
    col = lax.broadcasted_iota(jnp.int32, (gt, gt), 1)
    in_chunk_causal = (row // c_len == col // c_len) & (col <= row)
    tri = jnp.where(in_chunk_causal, 1.0, 0.0).astype(BF16)
    lane = lax.broadcasted_iota(jnp.int32, (1, LANES), 1)
    head_lanes = (jnp.where(lane < GLA_DK, 1.0, 0.0).astype(BF16), jnp.where(lane < GLA_DK, 0.0, 1.0).astype(BF16))
    row_chunk = lax.broadcasted_iota(jnp.int32, (gt, LANES), 0) // c_len
    zero_k = jnp.zeros((gt, LANES), BF16)
    a_parts, u_parts = [], []

    def gla_tile(r0):
        rows = slice(r0, r0 + gt)
        f_low = z_ref[rows, Z_F:Z_F + LANES].astype(BF16)
        gate_in = _dot(f_low, wf_ref[...]) + bf_ref[...]
        log_f = (jnp.minimum(gate_in, 0.0) - jnp.log1p(jnp.exp(-jnp.abs(gate_in)))) * (1.0 / GLA_GATE_TAU)
        hi, mid, lo = _split3(log_f)
        b_all = _dot(tri, hi) + _dot(tri, mid) + _dot(tri, lo)
        b_last = [b_all[(c + 1) * c_len - 1:(c + 1) * c_len] for c in range(n_chunks)]
        b_last_rows = jnp.concatenate([jnp.broadcast_to(bl, (c_len, GLA_KW)) for bl in b_last], axis=0)
        decay = [jnp.exp(bl) for bl in b_last]
        q_all = z_ref[rows, Z_Q:Z_Q + GLA_KW]
        k_all = z_ref[rows, Z_K:Z_K + GLA_KW]
        q_dec = (q_all * (GLA_DK ** -0.5) * jnp.exp(b_all)).astype(BF16)
        k_dec = (k_all * jnp.exp(-b_all)).astype(BF16)
        k_end = (k_all * jnp.exp(b_last_rows - b_all)).astype(BF16)
        for pair in range(GLA_HEADS // 2):
            kl = slice(pair * LANES, (pair + 1) * LANES)
            q_pair = q_dec[:, kl]
            q_heads = jnp.concatenate([q_pair * head_lanes[0], q_pair * head_lanes[1]], axis=0)
            scores = _dot_nt(q_heads, k_dec[:, kl])
            k_end_pair = k_end[:, kl]
            k_end_blocks = jnp.concatenate(
                [jnp.where(row_chunk == c, k_end_pair, zero_k) for c in range(n_chunks)], axis=1)
            for half in range(2):
                hd = 2 * pair + half
                vl = slice(hd * GLA_DV, (hd + 1) * GLA_DV)
                q_h = q_heads[half * gt:(half + 1) * gt]
                att = jnp.where(in_chunk_causal, scores[half * gt:(half + 1) * gt], 0.0).astype(BF16)
                v_h = z_ref[rows, Z_V + hd * GLA_DV:Z_V + (hd + 1) * GLA_DV].astype(BF16)
                o = _dot(att, v_h)
                kv_t = _dot_tn(v_h, k_end_blocks)
                s_t = st_ref[hd]
                o_inter = []
                for c in range(n_chunks):
                    o_inter.append(_dot_nt(q_h[c * c_len:(c + 1) * c_len], s_t.astype(BF16)))
                    s_t = s_t * decay[c][:, kl] + kv_t[:, c * LANES:(c + 1) * LANES]
                st_ref[hd] = s_t
                o = o + jnp.concatenate(o_inter, axis=0)
                o = o * lax.rsqrt(jnp.mean(o * o, axis=-1, keepdims=True) + EPS)
                r_h = z_ref[rows, Z_R + hd * GLA_DV:Z_R + (hd + 1) * GLA_DV]
                o_ref[rows, vl] = (o * gn_ref[:, vl] * (r_h * _sigmoid(r_h))).astype(o_ref.dtype)
                block = gt // GLA_HEADS
                a_blk, u_blk = lru_inputs(r0 + hd * block, block)
                a_parts.append(a_blk)
                u_parts.append(u_blk)

    for r0 in range(0, ts, gt):
        gla_tile(r0)

    a_all = jnp.concatenate(a_parts, axis=0)
    u_all = jnp.concatenate(u_parts, axis=0)
    srow = lax.broadcasted_iota(jnp.int32, (SUBLANES, LRU_WIDTH), 0)
    local = []
    for g in range(ts // SUBLANES):
        a = a_all[g * SUBLANES:(g + 1) * SUBLANES]
        u = u_all[g * SUBLANES:(g + 1) * SUBLANES]
        for s in (1, 2, 4):
            keep = srow >= s
            a_sh = jnp.where(keep, pltpu.roll(a, s, 0), 1.0)
            u_sh = jnp.where(keep, pltpu.roll(u, s, 0), 0.0)
            u = a * u_sh + u
            a = a * a_sh
        local.append((a, u))
    carry = hc_ref[...]
    h_groups = []
    for a, u in local:
        h_groups.append(a * carry + u)
        carry = a[SUBLANES - 1:SUBLANES, :] * carry + u[SUBLANES - 1:SUBLANES, :]
    hc_ref[...] = carry
    g_in = z_ref[:, Z_G:Z_G + LRU_WIDTH]
    gelu = 0.5 * g_in * (1.0 + jnp.tanh(math.sqrt(2.0 / math.pi) * (g_in + 0.044715 * (g_in * g_in * g_in))))
    o_ref[:, GLA_VW:GLA_VW + LRU_WIDTH] = (jnp.concatenate(h_groups, axis=0) * gelu).astype(o_ref.dtype)


def _hybrid_core(z, wf, bf, gn, cw, cb, wa, ba, wx, bx, lam):
    b, s, _ = z.shape
    const = lambda shape: pl.BlockSpec(shape, lambda i, j: (0,) * len(shape))
    return pl.pallas_call(
        _hybrid_body,
        grid=(b, s // HYB_TS),
        in_specs=[
            pl.BlockSpec((None, HYB_TS, HYB_ZW), lambda i, j: (i, j, 0)),
            const((LANES, GLA_KW)), const((1, GLA_KW)), const((1, GLA_VW)),
            const((LRU_CONV_W, LRU_WIDTH)), const((1, LRU_WIDTH)),
            const((2, LRU_WIDTH // 2, LRU_WIDTH // 2)), const((1, LRU_WIDTH)),
            const((2, LRU_WIDTH // 2, LRU_WIDTH // 2)), const((1, LRU_WIDTH)),
            const((1, LRU_WIDTH)),
        ],
        out_specs=pl.BlockSpec((None, HYB_TS, GLA_VW + LRU_WIDTH), lambda i, j: (i, j, 0)),
        out_shape=jax.ShapeDtypeStruct((b, s, GLA_VW + LRU_WIDTH), BF16),
        scratch_shapes=[
            pltpu.VMEM((GLA_HEADS, GLA_DV, LANES), F32),
            pltpu.VMEM((HYB_TS + 2 * SUBLANES, LRU_WIDTH), F32),
            pltpu.VMEM((1, LRU_WIDTH), F32),
        ],
        compiler_params=_params("parallel", "arbitrary"),
        name="hybrid_core",
    )(z, wf, bf, gn, cw, cb, wa, ba, wx, bx, lam)


def _block_diag_halves(w):
    g, bw, _ = w.shape
    eye = jnp.eye(g // 2, dtype=w.dtype)
    w = w.reshape(2, g // 2, bw, bw)
    return (eye[None, :, None, :, None] * w[:, :, :, None, :]).reshape(2, g * bw // 2, g * bw // 2)


def _hybrid_in_weights(w_in):
    q_w, k_w, v_w, r_w, f_w, xr_w, gr_w = jnp.split(
        w_in, np.cumsum([GLA_KW, GLA_KW, GLA_VW, GLA_VW, GLA_GATE_RANK, LRU_WIDTH]).tolist(), axis=-1)
    f_w = jnp.pad(f_w, ((0, 0), (0, 0), (0, LANES - GLA_GATE_RANK)))
    return jnp.concatenate([q_w, k_w, v_w, r_w, xr_w, gr_w, f_w], axis=-1).astype(BF16)


def _hybrid_mix(z, w_fup, b_f, gla_norm, conv_w, conv_b, w_a, b_a, w_x, b_x, lam, batch):
    t = z.shape[0]
    wf = jnp.pad(w_fup, ((0, LANES - GLA_GATE_RANK), (0, 0))).astype(BF16)
    mix = _hybrid_core(
        z.reshape(batch, t // batch, HYB_ZW), wf, b_f.reshape(1, GLA_KW), gla_norm.reshape(1, GLA_VW),
        conv_w, conv_b.reshape(1, LRU_WIDTH), _block_diag_halves(w_a).astype(BF16), b_a.reshape(1, LRU_WIDTH),
        _block_diag_halves(w_x).astype(BF16), b_x.reshape(1, LRU_WIDTH), lam.reshape(1, LRU_WIDTH))
    return mix.reshape(t, GLA_VW + LRU_WIDTH)


def _t5_bucket_table():
    max_exact = REL_BUCKETS // 2
    dist = np.arange(SWA_BLOCK)
    d = np.maximum(dist, 1).astype(np.float32)
    large = max_exact + (np.log(d / max_exact) / math.log(REL_MAX_DIST / max_exact)
                         * (REL_BUCKETS - max_exact)).astype(np.int32)
    bucket = np.where(dist < max_exact, dist, np.minimum(large, REL_BUCKETS - 1)).astype(np.int32)
    i = np.arange(SWA_BLOCK)[:, None]
    j = np.arange(SWA_BLOCK)[None, :]
    return bucket[(i - j) % SWA_BLOCK]


def _roll_half_lanes(x):
    words = pltpu.bitcast(x, jnp.int32)
    return pltpu.bitcast(pltpu.roll(words, LANES // 2, 1), BF16)


def _swa_body(rel_ref, sink_ref, bkt_ref, q_ref, kvc_ref, kvp_ref, o_ref, bias_ref, kd_ref, vo_ref):
    blk = SWA_BLOCK
    dh = SWA_HEAD_DIM
    tq = SWA_TQ
    first_tile = pl.program_id(1) == 0
    row = lax.broadcasted_iota(jnp.int32, (blk, blk), 0)
    col = lax.broadcasted_iota(jnp.int32, (blk, blk), 1)
    in_cur = col <= row
    lo_half = col < dh

    @pl.when((pl.program_id(0) == 0) & first_tile)
    def _():
        bkt = bkt_ref[...]
        for h in range(SWA_HEADS):
            acc = jnp.zeros((blk, blk), F32)
            for k in range(REL_BUCKETS):
                acc = jnp.where(bkt == k, rel_ref[k, h], acc)
            bias_ref[0, h] = acc
            bias_ref[1, h] = jnp.where(in_cur, acc, MASK_VALUE)
        band_lane = lax.broadcasted_iota(jnp.int32, (tq + blk, LANES), 1)
        ones_lo = jnp.where(band_lane < dh, 1.0, 0.0).astype(BF16)
        for kv in range(SWA_KV_HEADS):
            vo_ref[kv, 0, :, LANES:] = ones_lo
            vo_ref[kv, 1, :, LANES:] = 1.0 - ones_lo

    def build_band(src_ref, r0, rows):
        lo = lax.broadcasted_iota(jnp.int32, (rows, LANES), 1) < dh
        zero = jnp.zeros((rows, LANES), BF16)
        for kv in range(SWA_KV_HEADS):
            c0 = LANES * (kv // 2)
            k_pair = src_ref[:, c0:c0 + LANES]
            v_pair = src_ref[:, SWA_KV_W + c0:SWA_KV_W + c0 + LANES]
            k_rot = _roll_half_lanes(k_pair)
            v_rot = _roll_half_lanes(v_pair)
            k_own, k_other, v_own, v_other = ((k_pair, k_rot, v_pair, v_rot) if kv % 2 == 0
                                              else (k_rot, k_pair, v_rot, v_pair))
            kd_ref[kv, r0:r0 + rows, :] = jnp.where(lo, k_own, k_other)
            vo_ref[kv, 0, r0:r0 + rows, :LANES] = jnp.where(lo, v_own, zero)
            vo_ref[kv, 1, r0:r0 + rows, :LANES] = jnp.where(lo, zero, v_other)

    build_band(kvp_ref, 0, blk)
    build_band(kvc_ref, blk, tq)

    lane = lax.broadcasted_iota(jnp.int32, (1, LANES), 1)
    q_scale = (jnp.where(lane < dh, dh ** -0.5, 0.0).astype(BF16), jnp.where(lane < dh, 0.0, dh ** -0.5).astype(BF16))
    zero_p = jnp.zeros((blk, blk), BF16)

    def sub_block(n, carry):
        r0 = pl.multiple_of(n * blk, blk)
        first = jnp.where(first_tile & (n == 0), 1, 0)
        scores = []
        for kv in range(SWA_KV_HEADS):
            k_band = kd_ref[kv, pl.ds(r0, 2 * blk), :]
            for pr in range(SWA_GROUP // 2):
                q_pair = q_ref[pl.ds(r0, blk), pl.ds(LANES * (2 * kv + pr), LANES)]
                for half in range(2):
                    h = SWA_GROUP * kv + 2 * pr + half
                    s = _dot_nt(q_pair * q_scale[half], k_band)
                    scores.append(jnp.where(in_cur, s[:, blk:], s[:, :blk]) + bias_ref[first, h])
        maxes = [jnp.maximum(jnp.max(s, axis=-1, keepdims=True), sink_ref[h]) for h, s in enumerate(scores)]
        exps = [jnp.exp(s - m).astype(BF16) for s, m in zip(scores, maxes)]
        for kv in range(SWA_KV_HEADS):
            for pr in range(SWA_GROUP // 2):
                h0 = SWA_GROUP * kv + 2 * pr
                acc = None
                for half in range(2):
                    e = exps[h0 + half]
                    p_band = jnp.concatenate([jnp.where(in_cur, zero_p, e), jnp.where(in_cur, e, zero_p)], axis=1)
                    part = _dot(p_band, vo_ref[kv, half, pl.ds(r0, 2 * blk), :])
                    acc = part if acc is None else acc + part
                sink_term = jnp.where(lo_half, jnp.exp(sink_ref[h0] - maxes[h0]),
                                      jnp.exp(sink_ref[h0 + 1] - maxes[h0 + 1]))
                out = acc[:, :LANES] / (acc[:, LANES:] + sink_term)
                o_ref[pl.ds(r0, blk), pl.ds(LANES * (2 * kv + pr), LANES)] = out.astype(o_ref.dtype)
        return carry

    lax.fori_loop(0, tq // blk, sub_block, 0, unroll=2)


def _swa_core(z, rel_bias, sinks):
    b, s, _ = z.shape
    kv_col = SWA_Q_W // (2 * SWA_KV_W)
    blocks_per_tile = SWA_TQ // SWA_BLOCK
    smem = lambda: pl.BlockSpec(memory_space=pltpu.SMEM)
    return pl.pallas_call(
        _swa_body,
        grid=(b, s // SWA_TQ),
        in_specs=[
            smem(), smem(),
            pl.BlockSpec((SWA_BLOCK, SWA_BLOCK), lambda i, j: (0, 0)),
            pl.BlockSpec((None, SWA_TQ, SWA_Q_W), lambda i, j: (i, j, 0)),
            pl.BlockSpec((None, SWA_TQ, 2 * SWA_KV_W), lambda i, j: (i, j, kv_col)),
            pl.BlockSpec((None, SWA_BLOCK, 2 * SWA_KV_W),
                         lambda i, j: (i, jnp.maximum(j * blocks_per_tile - 1, 0), kv_col)),
        ],
        out_specs=pl.BlockSpec((None, SWA_TQ, SWA_Q_W), lambda i, j: (i, j, 0)),
        out_shape=jax.ShapeDtypeStruct((b, s, SWA_Q_W), BF16),
        scratch_shapes=[
            pltpu.VMEM((2, SWA_HEADS, SWA_BLOCK, SWA_BLOCK), F32),
            pltpu.VMEM((SWA_KV_HEADS, SWA_TQ + SWA_BLOCK, LANES), BF16),
            pltpu.VMEM((SWA_KV_HEADS, 2, SWA_TQ + SWA_BLOCK, 2 * LANES), BF16),
        ],
        compiler_params=_params("arbitrary", "arbitrary"),
        name="swa_core",
    )(rel_bias, sinks, jnp.asarray(_t5_bucket_table()), z, z, z)


def _swa_mix(z, rel_bias, sinks, batch):
    t = z.shape[0]
    return _swa_core(z.reshape(batch, t // batch, SWA_Q_W + 2 * SWA_KV_W), rel_bias, sinks).reshape(t, SWA_Q_W)


def kernel(x, p, rel_bias, final_norm, ffn1_norm, ffn1_w_gate, ffn1_w_up, ffn1_w_down, mix_norm, ffn2_norm,
           ffn2_w_gate, ffn2_w_up, ffn2_w_down, ple_norm, ple_w_proj, ple_w_gate, hyb_w_in, hyb_w_out, gla_w_fup,
           gla_b_f, gla_norm, lru_conv_w, lru_conv_b, lru_w_a, lru_b_a, lru_w_x, lru_b_x, lru_lambda, swa_w_qkv,
           swa_b_qkv, swa_w_o, swa_b_o, swa_sinks):
    batch, seq, _ = x.shape
    t = batch * seq
    x = x.reshape(t, D_MODEL)
    p = p.reshape(DEPTH, t, PLE_DIM)
    gains = lambda g: g.reshape(DEPTH, 1, D_MODEL)
    ffn1_f32 = (ffn1_w_gate, ffn1_w_up, ffn1_w_down)
    ffn2_f32 = (ffn2_w_gate, ffn2_w_up, ffn2_w_down)
    g1, g2 = gains(ffn1_norm), gains(ffn2_norm)
    ple = (gains(ple_norm), ple_w_gate.astype(BF16), ple_w_proj.astype(BF16))
    w_ffn = [w[0].astype(BF16) for w in ffn1_f32]
    mix_g = gains(mix_norm)
    final_g = final_norm.reshape(1, D_MODEL)
    hyb_in, hyb_out = _hybrid_in_weights(hyb_w_in), hyb_w_out.astype(BF16)
    n_hyb = hyb_w_in.shape[0]
    hyb_b_in, hyb_b_out = jnp.zeros((n_hyb, 1, HYB_ZW), F32), jnp.zeros((n_hyb, 1, D_MODEL), F32)
    swa_in, swa_out = swa_w_qkv.astype(BF16), swa_w_o.astype(BF16)
    swa_b_in, swa_b_out = swa_b_qkv[:, None, :], swa_b_o[:, None, :]
    next_ffn = lambda layer: (ffn1_f32, layer + 1) if layer + 1 < DEPTH else ((), 0)
    for i in range(DEPTH):
        if i % 2 == 0:
            e = i // 2
            x, z, w_ffn = _pre_stage(x, i, (g1, *w_ffn), mix_g, hyb_in, hyb_b_in, e, F32, PRE_HYB_TM,
                                     (ffn2_f32, i), "pre_hyb")
            mix = _hybrid_mix(z, gla_w_fup[e], gla_b_f[e], gla_norm[e], lru_conv_w[e], lru_conv_b[e], lru_w_a[e],
                              lru_b_a[e], lru_w_x[e], lru_b_x[e], lru_lambda[e], batch)
            x, w_ffn = _post_stage(mix, x, p, i, hyb_out, hyb_b_out, e, (g2, *w_ffn), ple, final_g, next_ffn(i),
                                   "post_hyb")
        else:
            o = i // 2
            x, z, w_ffn = _pre_stage(x, i, (g1, *w_ffn), mix_g, swa_in, swa_b_in, o, BF16, TOK_TM, (ffn2_f32, i),
                                     "pre_swa")
            mix = _swa_mix(z, rel_bias, swa_sinks[o], batch)
            x, w_ffn = _post_stage(mix, x, p, i, swa_out, swa_b_out, o, (g2, *w_ffn), ple, final_g, next_ffn(i),
                                   "post_swa")
    return x.reshape(batch, seq, D_MODEL)
```

```python
import functools
import math

import jax
import jax.numpy as jnp
import numpy as np
from jax import lax
from jax.experimental import pallas as pl
from jax.experimental.pallas import tpu as pltpu

F32 = jnp.float32
BF16 = jnp.bfloat16

D_MODEL = 1024
DEPTH = 4
PLE_DIM = 256
D_FF = 2816
EPS = 1e-6

GLA_HEADS = 4
GLA_DK = 64
GLA_DV = 128
GLA_KW = GLA_HEADS * GLA_DK
GLA_VW = GLA_HEADS * GLA_DV
GLA_GATE_RANK = 16
GLA_GATE_TAU = 16.0
GLA_CHUNK = 64
LRU_WIDTH = 512
LRU_BLOCKS = 8
LRU_CONV_W = 4
LRU_C = 8.0

SWA_HEADS = 16
SWA_KV_HEADS = 4
SWA_HEAD_DIM = 64
SWA_GROUP = SWA_HEADS // SWA_KV_HEADS
SWA_BLOCK = 128
SWA_Q_W = SWA_HEADS * SWA_HEAD_DIM
SWA_KV_W = SWA_KV_HEADS * SWA_HEAD_DIM
REL_BUCKETS = 32
REL_MAX_DIST = 128
MASK_VALUE = -1e30

LANES = 128
SUBLANES = 8
BF16_SUBLANES = 16
VMEM_LIMIT_BYTES = 56 * 1024 * 1024

TOK_TM = 512
PRE_HYB_TM = 512
PRE_SWA_TM = 1024
FFN_TF = 256
HYB_TS = 1024
GLA_TILE = 256
SWA_TQ = 1024

Z_Q = 0
Z_K = Z_Q + GLA_KW
Z_V = Z_K + GLA_KW
Z_R = Z_V + GLA_VW
Z_X = Z_R + GLA_VW
Z_G = Z_X + LRU_WIDTH
Z_F = Z_G + LRU_WIDTH
HYB_ZW = Z_F + LANES


def _params(*semantics):
    return pltpu.CompilerParams(dimension_semantics=semantics, vmem_limit_bytes=VMEM_LIMIT_BYTES)


def _rms(x, g):
    return x * lax.rsqrt(jnp.mean(x * x, axis=-1, keepdims=True) + EPS) * g


def _sigmoid(x):
    return 1.0 / (1.0 + jnp.exp(-x))


def _dot(a, b):
    return jnp.dot(a, b, preferred_element_type=F32)


def _dot_nt(a, b):
    return lax.dot_general(a, b, (((1,), (1,)), ((), ())), preferred_element_type=F32)


def _dot_tn(a, b):
    return lax.dot_general(a, b, (((0,), (0,)), ((), ())), preferred_element_type=F32)


def _ffn_apply(x, g_ref, wg_ref, wu_ref, wd_ref):
    xn = _rms(x, g_ref[...]).astype(BF16)
    acc = x
    blocks = [(c0, min(c0 + FFN_TF, D_FF)) for c0 in range(0, D_FF, FFN_TF)]
    gate_up = lambda blk: (_dot(xn, wg_ref[:, blk[0]:blk[1]]), _dot(xn, wu_ref[:, blk[0]:blk[1]]))
    nxt = gate_up(blocks[0])
    for idx, (c0, c1) in enumerate(blocks):
        a, b = nxt
        if idx + 1 < len(blocks):
            nxt = gate_up(blocks[idx + 1])
        h = (0.5 * a) * _sigmoid(a) * b
        acc = acc + _dot(h.astype(BF16), wd_ref[c0:c1, :])
    return acc


def _cast_next_weights(refs):
    half = len(refs) // 2
    for src_ref, dst_ref in zip(refs[:half], refs[half:]):
        dst_ref[...] = src_ref[...].astype(BF16)


def _pre_body(x_ref, g1_ref, wg_ref, wu_ref, wd_ref, gm_ref, win_ref, bin_ref, *rest):
    nxt_in, (xo_ref, z_ref), nxt_out = rest[:-5], rest[-5:-3], rest[-3:]
    _cast_next_weights(nxt_in + nxt_out)
    x = _ffn_apply(x_ref[...], g1_ref, wg_ref, wu_ref, wd_ref)
    xo_ref[...] = x
    z_ref[...] = (_dot(_rms(x, gm_ref[...]).astype(BF16), win_ref[...]) + bin_ref[...]).astype(z_ref.dtype)


def _post_body(mix_ref, x_ref, p_ref, wo_ref, bo_ref, g2_ref, wg_ref, wu_ref, wd_ref, gp_ref, wpg_ref, wpp_ref,
               fg_ref, *rest, final):
    n_next = (len(rest) - 1) // 2
    nxt_in, o_ref, nxt_out = rest[:n_next], rest[n_next], rest[n_next + 1:]
    _cast_next_weights(nxt_in + nxt_out)
    x = x_ref[...] + (_dot(mix_ref[...], wo_ref[...]) + bo_ref[...])
    x = _ffn_apply(x, g2_ref, wg_ref, wu_ref, wd_ref)
    gate = _sigmoid(_dot(_rms(x, gp_ref[...]).astype(BF16), wpg_ref[...]))
    x = x + gate * _dot(p_ref[...].astype(BF16), wpp_ref[...])
    if final:
        x = _rms(x, fg_ref[...])
    o_ref[...] = x


def _rows(width, tm=TOK_TM):
    return pl.BlockSpec((tm, width), lambda i: (i, 0))


def _resident(array, layer=None):
    if layer is None:
        return pl.BlockSpec(array.shape, lambda i: (0, 0), pipeline_mode=pl.Buffered(1))
    return pl.BlockSpec((None,) + array.shape[1:], lambda i: (layer, 0, 0), pipeline_mode=pl.Buffered(1))


def _cast_slices(weights, layer, steps):
    in_specs, out_specs, out_shapes = [], [], []
    for w in weights:
        _, rows, cols = w.shape
        span = 1 if (rows // steps) % BF16_SUBLANES == 0 else 2
        block = rows * span // steps
        assert rows % block == 0 and block % BF16_SUBLANES == 0
        in_specs.append(pl.BlockSpec((None, block, cols), lambda i, span=span: (layer, i // span, 0)))
        out_specs.append(pl.BlockSpec((block, cols), lambda i, span=span: (i // span, 0)))
        out_shapes.append(jax.ShapeDtypeStruct((rows, cols), BF16))
    return in_specs, out_specs, out_shapes


def _ffn_specs(ffn, layer):
    return [_resident(ffn[0], layer)] + [_resident(w) for w in ffn[1:]]


def _pre_stage(x, layer, ffn, mix_g, w_in, b_in, mixer, z_dtype, tm, next_ffn, name):
    t = x.shape[0]
    n = w_in.shape[-1]
    nxt_w, nxt_layer = next_ffn
    cast_in, cast_out, cast_shapes = _cast_slices(nxt_w, nxt_layer, t // tm)
    x, z, *nxt = pl.pallas_call(
        _pre_body,
        grid=(t // tm,),
        in_specs=[_rows(D_MODEL, tm)] + _ffn_specs(ffn, layer)
        + [_resident(mix_g, layer), _resident(w_in, mixer), _resident(b_in, mixer)] + cast_in,
        out_specs=[_rows(D_MODEL, tm), _rows(n, tm)] + cast_out,
        out_shape=[jax.ShapeDtypeStruct((t, D_MODEL), F32), jax.ShapeDtypeStruct((t, n), z_dtype)] + cast_shapes,
        compiler_params=_params("arbitrary"),
        name=name,
    )(x, *ffn, mix_g, w_in, b_in, *nxt_w)
    return x, z, nxt


def _post_stage(mix, x, p, layer, w_o, b_o, mixer, ffn, ple, final_g, next_ffn, name):
    t = x.shape[0]
    nxt_w, nxt_layer = next_ffn
    cast_in, cast_out, cast_shapes = _cast_slices(nxt_w, nxt_layer, t // TOK_TM)
    x, *nxt = pl.pallas_call(
        functools.partial(_post_body, final=(layer == DEPTH - 1)),
        grid=(t // TOK_TM,),
        in_specs=[_rows(mix.shape[1]), _rows(D_MODEL),
                  pl.BlockSpec((None, TOK_TM, PLE_DIM), lambda i: (layer, i, 0)),
                  _resident(w_o, mixer), _resident(b_o, mixer)]
        + _ffn_specs(ffn, layer) + [_resident(w, layer) for w in ple] + [_resident(final_g)] + cast_in,
        out_specs=[_rows(D_MODEL)] + cast_out,
        out_shape=[jax.ShapeDtypeStruct((t, D_MODEL), F32)] + cast_shapes,
        compiler_params=_params("arbitrary"),
        name=name,
    )(mix, x, p, w_o, b_o, *ffn, *ple, final_g, *nxt_w)
    return x, nxt


def _split3(x):
    hi = x.astype(BF16)
    r1 = x - hi.astype(F32)
    mid = r1.astype(BF16)
    lo = (r1 - mid.astype(F32)).astype(BF16)
    return hi, mid, lo


def _hybrid_body(z_ref, wf_ref, bf_ref, gn_ref, cw_ref, cb_ref, wa_ref, ba_ref, wx_ref, bx_ref, lam_ref,
                 o_ref, st_ref, xbuf_ref, hc_ref):
    ts = HYB_TS
    gt = GLA_TILE
    c_len = GLA_CHUNK
    n_chunks = gt // c_len
    half_w = LRU_WIDTH // 2

    @pl.when(pl.program_id(1) == 0)
    def _():
        st_ref[...] = jnp.zeros_like(st_ref)
        xbuf_ref[0:SUBLANES, :] = jnp.zeros((SUBLANES, LRU_WIDTH), F32)
        hc_ref[...] = jnp.zeros_like(hc_ref)

    xbuf_ref[SUBLANES:SUBLANES + ts, :] = z_ref[:, Z_X:Z_X + LRU_WIDTH]
    taps_back = LRU_CONV_W - 1
    xc = xbuf_ref[SUBLANES - taps_back:SUBLANES - taps_back + ts, :] * cw_ref[0:1, :]
    for tap in range(1, LRU_CONV_W):
        off = SUBLANES - taps_back + tap
        xc = xc + xbuf_ref[off:off + ts, :] * cw_ref[tap:tap + 1, :]
    xc = xc + cb_ref[...]
    xbuf_ref[0:SUBLANES, :] = xbuf_ref[ts:ts + SUBLANES, :]
    xcb = xc.astype(BF16)
    r_pre = [_dot(xcb[:, d * half_w:(d + 1) * half_w], wa_ref[d]) for d in range(2)]
    i_pre = [_dot(xcb[:, d * half_w:(d + 1) * half_w], wx_ref[d]) for d in range(2)]
    neg_lam = -lam_ref[...]
    softplus = jnp.maximum(neg_lam, 0.0) + jnp.log1p(jnp.exp(-jnp.abs(neg_lam)))

    def lru_inputs(r0, rows):
        rs = slice(r0, r0 + rows)
        r_gate = _sigmoid(jnp.concatenate([r_pre[0][rs], r_pre[1][rs]], axis=1) + ba_ref[...])
        i_gate = _sigmoid(jnp.concatenate([i_pre[0][rs], i_pre[1][rs]], axis=1) + bx_ref[...])
        log_a = -LRU_C * r_gate * softplus
        th = jnp.tanh(log_a)
        return jnp.exp(log_a), jnp.sqrt(-2.0 * th / (1.0 - th)) * (i_gate * xc[rs])

    row = lax.broadcasted_iota(jnp.int32, (gt, gt), 0)
    col = lax.broadcasted_iota(jnp.int32, (gt, gt), 1)
    in_chunk_causal = (row // c_len == col // c_len) & (col <= row)
    tri = jnp.where(in_chunk_causal, 1.0, 0.0).astype(BF16)
    lane = lax.broadcasted_iota(jnp.int32, (1, LANES), 1)
    head_lanes = (jnp.where(lane < GLA_DK, 1.0, 0.0).astype(BF16), jnp.where(lane < GLA_DK, 0.0, 1.0).astype(BF16))
    row_chunk = lax.broadcasted_iota(jnp.int32, (gt, LANES), 0) // c_len
    zero_k = jnp.zeros((gt, LANES), BF16)
    a_parts, u_parts = [], []

    def gla_tile(r0):
        rows = slice(r0, r0 + gt)
        f_low = z_ref[rows, Z_F:Z_F + LANES].astype(BF16)
        gate_in = _dot(f_low, wf_ref[...]) + bf_ref[...]
        log_f = (jnp.minimum(gate_in, 0.0) - jnp.log1p(jnp.exp(-jnp.abs(gate_in)))) * (1.0 / GLA_GATE_TAU)
        hi, mid, lo = _split3(log_f)
        b_all = _dot(tri, hi) + _dot(tri, mid) + _dot(tri, lo)
        b_last = [b_all[(c + 1) * c_len - 1:(c + 1) * c_len] for c in range(n_chunks)]
        b_last_rows = jnp.concatenate([jnp.broadcast_to(bl, (c_len, GLA_KW)) for bl in b_last], axis=0)
        decay = [jnp.exp(bl) for bl in b_last]
        q_all = z_ref[rows, Z_Q:Z_Q + GLA_KW]
        k_all = z_ref[rows, Z_K:Z_K + GLA_KW]
        q_dec = (q_all * (GLA_DK ** -0.5) * jnp.exp(b_all)).astype(BF16)
        k_dec = (k_all * jnp.exp(-b_all)).astype(BF16)
        k_end = (k_all * jnp.exp(b_last_rows - b_all)).astype(BF16)
        for pair in range(GLA_HEADS // 2):
            kl = slice(pair * LANES, (pair + 1) * LANES)
            q_pair = q_dec[:, kl]
            q_heads = jnp.concatenate([q_pair * head_lanes[0], q_pair * head_lanes[1]], axis=0)
            scores = _dot_nt(q_heads, k_dec[:, kl])
            k_end_pair = k_end[:, kl]
            k_end_blocks = jnp.concatenate(
                [jnp.where(row_chunk == c, k_end_pair, zero_k) for c in range(n_chunks)], axis=1)
            for half in range(2):
                hd = 2 * pair + half
                vl = slice(hd * GLA_DV, (hd + 1) * GLA_DV)
                q_h = q_heads[half * gt:(half + 1) * gt]
                att = jnp.where(in_chunk_causal, scores[half * gt:(half + 1) * gt], 0.0).astype(BF16)
                v_h = z_ref[rows, Z_V + hd * GLA_DV:Z_V + (hd + 1) * GLA_DV].astype(BF16)
                o = _dot(att, v_h)
                kv_t = _dot_tn(v_h, k_end_blocks)
                s_t = st_ref[hd]
                o_inter = []
                for c in range(n_chunks):
                    o_inter.append(_dot_nt(q_h[c * c_len:(c + 1) * c_len], s_t.astype(BF16)))
                    s_t = s_t * decay[c][:, kl] + kv_t[:, c * LANES:(c + 1) * LANES]
                st_ref[hd] = s_t
                o = o + jnp.concatenate(o_inter, axis=0)
                o = o * lax.rsqrt(jnp.mean(o * o, axis=-1, keepdims=True) + EPS)
                r_h = z_ref[rows, Z_R + hd * GLA_DV:Z_R + (hd + 1) * GLA_DV]
                o_ref[rows, vl] = (o * gn_ref[:, vl] * (r_h * _sigmoid(r_h))).astype(o_ref.dtype)
                block = gt // GLA_HEADS
                a_blk, u_blk = lru_inputs(r0 + hd * block, block)
                a_parts.append(a_blk)
                u_parts.append(u_blk)

    for r0 in range(0, ts, gt):
        gla_tile(r0)

    a_all = jnp.concatenate(a_parts, axis=0)
    u_all = jnp.concatenate(u_parts, axis=0)
    srow = lax.broadcasted_iota(jnp.int32, (SUBLANES, LRU_WIDTH), 0)
    local = []
    for g in range(ts // SUBLANES):
        a = a_all[g * SUBLANES:(g + 1) * SUBLANES]
        u = u_all[g * SUBLANES:(g + 1) * SUBLANES]
        for s in (1, 2, 4):
            keep = srow >= s
            a_sh = jnp.where(keep, pltpu.roll(a, s, 0), 1.0)
            u_sh = jnp.where(keep, pltpu.roll(u, s, 0), 0.0)
            u = a * u_sh + u
            a = a * a_sh
        local.append((a, u))
    carry = hc_ref[...]
    h_groups = []
    for a, u in local:
        h_groups.append(a * carry + u)
        carry = a[SUBLANES - 1:SUBLANES, :] * carry + u[SUBLANES - 1:SUBLANES, :]
    hc_ref[...] = carry
    g_in = z_ref[:, Z_G:Z_G + LRU_WIDTH]
    gelu = 0.5 * g_in * (1.0 + jnp.tanh(math.sqrt(2.0 / math.pi) * (g_in + 0.044715 * (g_in * g_in * g_in))))
    o_ref[:, GLA_VW:GLA_VW + LRU_WIDTH] = (jnp.concatenate(h_groups, axis=0) * gelu).astype(o_ref.dtype)


def _hybrid_core(z, wf, bf, gn, cw, cb, wa, ba, wx, bx, lam):
    b, s, _ = z.shape
    const = lambda shape: pl.BlockSpec(shape, lambda i, j: (0,) * len(shape))
    return pl.pallas_call(
        _hybrid_body,
        grid=(b, s // HYB_TS),
        in_specs=[
            pl.BlockSpec((None, HYB_TS, HYB_ZW), lambda i, j: (i, j, 0)),
            const((LANES, GLA_KW)), const((1, GLA_KW)), const((1, GLA_VW)),
            const((LRU_CONV_W, LRU_WIDTH)), const((1, LRU_WIDTH)),
            const((2, LRU_WIDTH // 2, LRU_WIDTH // 2)), const((1, LRU_WIDTH)),
            const((2, LRU_WIDTH // 2, LRU_WIDTH // 2)), const((1, LRU_WIDTH)),
            const((1, LRU_WIDTH)),
        ],
        out_specs=pl.BlockSpec((None, HYB_TS, GLA_VW + LRU_WIDTH), lambda i, j: (i, j, 0)),
        out_shape=jax.ShapeDtypeStruct((b, s, GLA_VW + LRU_WIDTH), BF16),
        scratch_shapes=[
            pltpu.VMEM((GLA_HEADS, GLA_DV, LANES), F32),
            pltpu.VMEM((HYB_TS + 2 * SUBLANES, LRU_WIDTH), F32),
            pltpu.VMEM((1, LRU_WIDTH), F32),
        ],
        compiler_params=_params("parallel", "arbitrary"),
        name="hybrid_core",
    )(z, wf, bf, gn, cw, cb, wa, ba, wx, bx, lam)


def _block_diag_halves(w):
    g, bw, _ = w.shape
    eye = jnp.eye(g // 2, dtype=w.dtype)
    w = w.reshape(2, g // 2, bw, bw)
    return (eye[None, :, None, :, None] * w[:, :, :, None, :]).reshape(2, g * bw // 2, g * bw // 2)


def _hybrid_in_weights(w_in):
    q_w, k_w, v_w, r_w, f_w, xr_w, gr_w = jnp.split(
        w_in, np.cumsum([GLA_KW, GLA_KW, GLA_VW, GLA_VW, GLA_GATE_RANK, LRU_WIDTH]).tolist(), axis=-1)
    f_w = jnp.pad(f_w, ((0, 0), (0, 0), (0, LANES - GLA_GATE_RANK)))
    return jnp.concatenate([q_w, k_w, v_w, r_w, xr_w, gr_w, f_w], axis=-1).astype(BF16)


def _hybrid_mix(z, w_fup, b_f, gla_norm, conv_w, conv_b, w_a, b_a, w_x, b_x, lam, batch):
    t = z.shape[0]
    wf = jnp.pad(w_fup, ((0, LANES - GLA_GATE_RANK), (0, 0))).astype(BF16)
    mix = _hybrid_core(
        z.reshape(batch, t // batch, HYB_ZW), wf, b_f.reshape(1, GLA_KW), gla_norm.reshape(1, GLA_VW),
        conv_w, conv_b.reshape(1, LRU_WIDTH), _block_diag_halves(w_a).astype(BF16), b_a.reshape(1, LRU_WIDTH),
        _block_diag_halves(w_x).astype(BF16), b_x.reshape(1, LRU_WIDTH), lam.reshape(1, LRU_WIDTH))
    return mix.reshape(t, GLA_VW + LRU_WIDTH)


def _t5_bucket_table():
    max_exact = REL_BUCKETS // 2
    dist = np.arange(SWA_BLOCK)
    d = np.maximum(dist, 1).astype(np.float32)
    large = max_exact + (np.log(d / max_exact) / math.log(REL_MAX_DIST / max_exact)
                         * (REL_BUCKETS - max_exact)).astype(np.int32)
    bucket = np.where(dist < max_exact, dist, np.minimum(large, REL_BUCKETS - 1)).astype(np.int32)
    i = np.arange(SWA_BLOCK)[:, None]
    j = np.arange(SWA_BLOCK)[None, :]
    return bucket[(i - j) % SWA_BLOCK]


def _roll_half_lanes(x):
    words = pltpu.bitcast(x, jnp.int32)
    return pltpu.bitcast(pltpu.roll(words, LANES // 2, 1), BF16)


def _swa_body(rel_ref, sink_ref, bkt_ref, q_ref, kvc_ref, kvp_ref, o_ref, bias_ref, kd_ref, vo_ref):
    blk = SWA_BLOCK
    dh = SWA_HEAD_DIM
    tq = SWA_TQ
    first_tile = pl.program_id(1) == 0
    row = lax.broadcasted_iota(jnp.int32, (blk, blk), 0)
    col = lax.broadcasted_iota(jnp.int32, (blk, blk), 1)
    in_cur = col <= row
    lo_half = col < dh

    @pl.when((pl.program_id(0) == 0) & first_tile)
    def _():
        bkt = bkt_ref[...]
        for h in range(SWA_HEADS):
            acc = jnp.zeros((blk, blk), F32)
            for k in range(REL_BUCKETS):
                acc = jnp.where(bkt == k, rel_ref[k, h], acc)
            bias_ref[0, h] = acc
            bias_ref[1, h] = jnp.where(in_cur, acc, MASK_VALUE)
        band_lane = lax.broadcasted_iota(jnp.int32, (tq + blk, LANES), 1)
        ones_lo = jnp.where(band_lane < dh, 1.0, 0.0).astype(BF16)
        for kv in range(SWA_KV_HEADS):
            vo_ref[kv, 0, :, LANES:] = ones_lo
            vo_ref[kv, 1, :, LANES:] = 1.0 - ones_lo

    def build_band(src_ref, r0, rows):
        lo = lax.broadcasted_iota(jnp.int32, (rows, LANES), 1) < dh
        zero = jnp.zeros((rows, LANES), BF16)
        for kv in range(SWA_KV_HEADS):
            c0 = LANES * (kv // 2)
            k_pair = src_ref[:, c0:c0 + LANES]
            v_pair = src_ref[:, SWA_KV_W + c0:SWA_KV_W + c0 + LANES]
            k_rot = _roll_half_lanes(k_pair)
            v_rot = _roll_half_lanes(v_pair)
            k_own, k_other, v_own, v_other = ((k_pair, k_rot, v_pair, v_rot) if kv % 2 == 0
                                              else (k_rot, k_pair, v_rot, v_pair))
            kd_ref[kv, r0:r0 + rows, :] = jnp.where(lo, k_own, k_other)
            vo_ref[kv, 0, r0:r0 + rows, :LANES] = jnp.where(lo, v_own, zero)
            vo_ref[kv, 1, r0:r0 + rows, :LANES] = jnp.where(lo, zero, v_other)

    build_band(kvp_ref, 0, blk)
    build_band(kvc_ref, blk, tq)

    lane = lax.broadcasted_iota(jnp.int32, (1, LANES), 1)
    q_scale = (jnp.where(lane < dh, dh ** -0.5, 0.0).astype(BF16), jnp.where(lane < dh, 0.0, dh ** -0.5).astype(BF16))
    zero_p = jnp.zeros((blk, blk), BF16)

    def sub_block(n, carry):
        r0 = pl.multiple_of(n * blk, blk)
        first = jnp.where(first_tile & (n == 0), 1, 0)
        scores = []
        for kv in range(SWA_KV_HEADS):
            k_band = kd_ref[kv, pl.ds(r0, 2 * blk), :]
            for pr in range(SWA_GROUP // 2):
                q_pair = q_ref[pl.ds(r0, blk), pl.ds(LANES * (2 * kv + pr), LANES)]
                for half in range(2):
                    h = SWA_GROUP * kv + 2 * pr + half
                    s = _dot_nt(q_pair * q_scale[half], k_band)
                    scores.append(jnp.where(in_cur, s[:, blk:], s[:, :blk]) + bias_ref[first, h])
        maxes = [jnp.maximum(jnp.max(s, axis=-1, keepdims=True), sink_ref[h]) for h, s in enumerate(scores)]
        exps = [jnp.exp(s - m).astype(BF16) for s, m in zip(scores, maxes)]
        for kv in range(SWA_KV_HEADS):
            for pr in range(SWA_GROUP // 2):
                h0 = SWA_GROUP * kv + 2 * pr
                acc = None
                for half in range(2):
                    e = exps[h0 + half]
                    p_band = jnp.concatenate([jnp.where(in_cur, zero_p, e), jnp.where(in_cur, e, zero_p)], axis=1)
                    part = _dot(p_band, vo_ref[kv, half, pl.ds(r0, 2 * blk), :])
                    acc = part if acc is None else acc + part
                sink_term = jnp.where(lo_half, jnp.exp(sink_ref[h0] - maxes[h0]),
                                      jnp.exp(sink_ref[h0 + 1] - maxes[h0 + 1]))
                out = acc[:, :LANES] / (acc[:, LANES:] + sink_term)
                o_ref[pl.ds(r0, blk), pl.ds(LANES * (2 * kv + pr), LANES)] = out.astype(o_ref.dtype)
        return carry

    lax.fori_loop(0, tq // blk, sub_block, 0, unroll=4)


def _swa_core(z, rel_bias, sinks):
    b, s, _ = z.shape
    kv_col = SWA_Q_W // (2 * SWA_KV_W)
    blocks_per_tile = SWA_TQ // SWA_BLOCK
    smem = lambda: pl.BlockSpec(memory_space=pltpu.SMEM)
    return pl.pallas_call(
        _swa_body,
        grid=(b, s // SWA_TQ),
        in_specs=[
            smem(), smem(),
            pl.BlockSpec((SWA_BLOCK, SWA_BLOCK), lambda i, j: (0, 0)),
            pl.BlockSpec((None, SWA_TQ, SWA_Q_W), lambda i, j: (i, j, 0)),
            pl.BlockSpec((None, SWA_TQ, 2 * SWA_KV_W), lambda i, j: (i, j, kv_col)),
            pl.BlockSpec((None, SWA_BLOCK, 2 * SWA_KV_W),
                         lambda i, j: (i, jnp.maximum(j * blocks_per_tile - 1, 0), kv_col)),
        ],
        out_specs=pl.BlockSpec((None, SWA_TQ, SWA_Q_W), lambda i, j: (i, j, 0)),
        out_shape=jax.ShapeDtypeStruct((b, s, SWA_Q_W), BF16),
        scratch_shapes=[
            pltpu.VMEM((2, SWA_HEADS, SWA_BLOCK, SWA_BLOCK), F32),
            pltpu.VMEM((SWA_KV_HEADS, SWA_TQ + SWA_BLOCK, LANES), BF16),
            pltpu.VMEM((SWA_KV_HEADS, 2, SWA_TQ + SWA_BLOCK, 2 * LANES), BF16),
        ],
        compiler_params=_params("arbitrary", "arbitrary"),
        name="swa_core",
    )(rel_bias, sinks, jnp.asarray(_t5_bucket_table()), z, z, z)


def _swa_mix(z, rel_bias, sinks, batch):
    t = z.shape[0]
    return _swa_core(z.reshape(batch, t // batch, SWA_Q_W + 2 * SWA_KV_W), rel_bias, sinks).reshape(t, SWA_Q_W)


def kernel(x, p, rel_bias, final_norm, ffn1_norm, ffn1_w_gate, ffn1_w_up, ffn1_w_down, mix_norm, ffn2_norm,
           ffn2_w_gate, ffn2_w_up, ffn2_w_down, ple_norm, ple_w_proj, ple_w_gate, hyb_w_in, hyb_w_out, gla_w_fup,
           gla_b_f, gla_norm, lru_conv_w, lru_conv_b, lru_w_a, lru_b_a, lru_w_x, lru_b_x, lru_lambda, swa_w_qkv,
           swa_b_qkv, swa_w_o, swa_b_o, swa_sinks):
    batch, seq, _ = x.shape
    t = batch * seq
    x = x.reshape(t, D_MODEL)
    p = p.reshape(DEPTH, t, PLE_DIM)
    gains = lambda g: g.reshape(DEPTH, 1, D_MODEL)
    ffn1_f32 = (ffn1_w_gate, ffn1_w_up, ffn1_w_down)
    ffn2_f32 = (ffn2_w_gate, ffn2_w_up, ffn2_w_down)
    g1, g2 = gains(ffn1_norm), gains(ffn2_norm)
    ple = (gains(ple_norm), ple_w_gate.astype(BF16), ple_w_proj.astype(BF16))
    w_ffn = [w[0].astype(BF16) for w in ffn1_f32]
    mix_g = gains(mix_norm)
    final_g = final_norm.reshape(1, D_MODEL)
    hyb_in, hyb_out = _hybrid_in_weights(hyb_w_in), hyb_w_out.astype(BF16)
    n_hyb = hyb_w_in.shape[0]
    hyb_b_in, hyb_b_out = jnp.zeros((n_hyb, 1, HYB_ZW), F32), jnp.zeros((n_hyb, 1, D_MODEL), F32)
    swa_in, swa_out = swa_w_qkv.astype(BF16), swa_w_o.astype(BF16)
    swa_b_in, swa_b_out = swa_b_qkv[:, None, :], swa_b_o[:, None, :]
    next_ffn = lambda layer: (ffn1_f32, layer + 1) if layer + 1 < DEPTH else ((), 0)
    for i in range(DEPTH):
        if i % 2 == 0:
            e = i // 2
            x, z, w_ffn = _pre_stage(x, i, (g1, *w_ffn), mix_g, hyb_in, hyb_b_in, e, F32, PRE_HYB_TM,
                                     (ffn2_f32, i), "pre_hyb")
            mix = _hybrid_mix(z, gla_w_fup[e], gla_b_f[e], gla_norm[e], lru_conv_w[e], lru_conv_b[e], lru_w_a[e],
                              lru_b_a[e], lru_w_x[e], lru_b_x[e], lru_lambda[e], batch)
            x, w_ffn = _post_stage(mix, x, p, i, hyb_out, hyb_b_out, e, (g2, *w_ffn), ple, final_g, next_ffn(i),
                                   "post_hyb")
        else:
            o = i // 2
            x, z, w_ffn = _pre_stage(x, i, (g1, *w_ffn), mix_g, swa_in, swa_b_in, o, BF16, PRE_SWA_TM,
                                     (ffn2_f32, i), "pre_swa")
            mix = _swa_mix(z, rel_bias, swa_sinks[o], batch)
            x, w_ffn = _post_stage(mix, x, p, i, swa_out, swa_b_out, o, (g2, *w_ffn), ple, final_g, next_ffn(i),
                                   "post_swa")
    return x.reshape(batch, seq, D_MODEL)
```
